```python
import math
import jax, jax.numpy as jnp
from jax import lax
import numpy as np

D_MODEL = 1024
BATCH = 2
SEQ = 8192
DEPTH = 2
DEC_BATCH = 32
DEC_SEQ = 4
PAST_LEN = 16384
PAGE_SIZE = 128

N_A = DEPTH // 2
N_B = DEPTH - N_A
EXPAND = 2
D_INNER = EXPAND * D_MODEL
SSM_HEAD_DIM = 64
SSM_HEADS = D_INNER // SSM_HEAD_DIM
SSM_GROUPS = 4
HEADS_PER_GROUP = SSM_HEADS // SSM_GROUPS
D_STATE = 128
CONV_W = 4
CONV_DIM = D_INNER + 2 * SSM_GROUPS * D_STATE
IN_PROJ_DIM = D_INNER + CONV_DIM + SSM_HEADS
SSD_CHUNK = 128
ATT_HEADS = 16
ATT_HEAD_DIM = D_MODEL // ATT_HEADS
ATT_DIM = ATT_HEADS * ATT_HEAD_DIM
Q_BLOCK = 128
FORGET_BIAS_LO = 2.0
FORGET_BIAS_HI = 12.0
D_FF = 2816
FFN_CONV_W = 3
EPS = 1e-6
F32 = jnp.float32

kernel_name = 'yoco_ssd_fox_convffn_step'


def rmsnorm(x, w):
    xf = x.astype(F32)
    y = xf * lax.rsqrt(jnp.mean(xf * xf, axis=-1, keepdims=True) + EPS)
    return (y * w.astype(F32)).astype(x.dtype)


def causal_dwconv(x, buf, w, b):
    L = x.shape[1]
    width = w.shape[0]
    xp = jnp.concatenate([buf.astype(x.dtype), x], axis=1)
    out = b.astype(x.dtype) + xp[:, 0:L] * w[0]
    for k in range(1, width):
        out = out + xp[:, k:k + L] * w[k]
    return out, xp[:, L:]


def ssd_chunked(x, dt, a, bm, cm, h0, chunk):
    b, L, G, R, P = x.shape
    N = bm.shape[-1]
    nc = L // chunk
    xc = x.reshape(b, nc, chunk, G, R, P)
    dtc = dt.reshape(b, nc, chunk, G, R)
    bc = bm.reshape(b, nc, chunk, G, N)
    cc = cm.reshape(b, nc, chunk, G, N)
    acum = jnp.cumsum(dtc * a, axis=2)
    seg = acum[:, :, :, None] - acum[:, :, None, :]
    causal = jnp.tril(jnp.ones((chunk, chunk), bool))[None, None, :, :, None, None]
    decay = jnp.exp(jnp.where(causal, seg, -jnp.inf))
    xdt = xc * dtc[..., None]
    cb = jnp.einsum('bclgn,bcsgn->bclsg', cc, bc)
    y_diag = jnp.einsum('bclsg,bclsgr,bcsgrp->bclgrp', cb, decay, xdt)
    decay_end = jnp.exp(acum[:, :, -1:] - acum)
    chunk_states = jnp.einsum('bclgn,bclgr,bclgrp->bcgrpn', bc, decay_end, xdt)
    chunk_decay = jnp.exp(acum[:, :, -1])

    def step(h, inp):
        s, d = inp
        return d[..., None, None] * h + s, h

    h_last, h_in = lax.scan(step, h0, (jnp.moveaxis(chunk_states, 1, 0), jnp.moveaxis(chunk_decay, 1, 0)))
    h_in = jnp.moveaxis(h_in, 0, 1)
    y_off = jnp.einsum('bclgn,bcgrpn,bclgr->bclgrp', cc, h_in, jnp.exp(acum))
    return (y_diag + y_off).reshape(b, L, G, R, P), h_last


def mamba2_mixer(u, conv_buf, ssm_state, w_in, conv_w, conv_b, dt_bias, a_log, d_skip, gnorm_w, w_out):
    bsz, L, _ = u.shape
    G, R, P, N = SSM_GROUPS, HEADS_PER_GROUP, SSM_HEAD_DIM, D_STATE
    zxbcdt = u @ w_in
    z = zxbcdt[..., :D_INNER]
    xbc = zxbcdt[..., D_INNER:D_INNER + CONV_DIM]
    dt_raw = zxbcdt[..., D_INNER + CONV_DIM:]
    xbc, new_conv = causal_dwconv(xbc, conv_buf, conv_w, conv_b)
    xbc = jax.nn.silu(xbc)
    xs = xbc[..., :D_INNER].reshape(bsz, L, G, R, P).astype(F32)
    bm = xbc[..., D_INNER:D_INNER + G * N].reshape(bsz, L, G, N).astype(F32)
    cm = xbc[..., D_INNER + G * N:].reshape(bsz, L, G, N).astype(F32)
    dt = jax.nn.softplus(dt_raw.astype(F32) + dt_bias.astype(F32)).reshape(bsz, L, G, R)
    a = -jnp.exp(a_log.astype(F32)).reshape(G, R)
    chunk = SSD_CHUNK if L % SSD_CHUNK == 0 else L
    h0 = ssm_state.astype(F32).reshape(bsz, G, R, P, N)
    y, h_last = ssd_chunked(xs, dt, a, bm, cm, h0, chunk)
    y = y + d_skip.astype(F32).reshape(G, R)[:, :, None] * xs
    y = y.reshape(bsz, L, D_INNER).astype(u.dtype)
    y = rmsnorm(y * jax.nn.silu(z), gnorm_w)
    return y @ w_out, new_conv, h_last.reshape(bsz, SSM_HEADS, P, N)


def conv_ffn(x, buf, w_in, conv_w, conv_b, w_out):
    gu = x @ w_in
    g, up = gu[..., :D_FF], gu[..., D_FF:]
    g, new_buf = causal_dwconv(g, buf, conv_w, conv_b)
    return (jax.nn.silu(g) * up) @ w_out, new_buf


def shared_kvf(h, kv_norm_w, w_kvf, b_f):
    bsz, L, _ = h.shape
    kvf = rmsnorm(h, kv_norm_w) @ w_kvf
    k = kvf[..., :ATT_DIM].reshape(bsz, L, ATT_HEADS, ATT_HEAD_DIM)
    v = kvf[..., ATT_DIM:2 * ATT_DIM].reshape(bsz, L, ATT_HEADS, ATT_HEAD_DIM)
    logf = jax.nn.log_sigmoid(kvf[..., 2 * ATT_DIM:].astype(F32) + b_f.astype(F32))
    return k, v, logf


def fox_prompt(q, k, v, logf):
    b, S, H, Dh = q.shape
    nq = S // Q_BLOCK
    scale = Dh ** -0.5
    F = jnp.cumsum(logf.astype(F32), axis=1).transpose(0, 2, 1)
    qb = jnp.moveaxis(q.reshape(b, nq, Q_BLOCK, H, Dh), 1, 0)
    fq = jnp.moveaxis(F.reshape(b, H, nq, Q_BLOCK), 2, 0)
    key_pos = jnp.arange(S)

    def block(args):
        qi, fi, i = args
        s = jnp.einsum('bqhd,bkhd->bhqk', qi, k).astype(F32) * scale
        s = s + fi[..., :, None] - F[:, :, None, :]
        q_pos = i * Q_BLOCK + jnp.arange(Q_BLOCK)
        s = jnp.where((q_pos[:, None] >= key_pos[None, :])[None, None], s, -jnp.inf)
        p = jax.nn.softmax(s, axis=-1)
        return jnp.einsum('bhqk,bkhd->bqhd', p.astype(v.dtype), v)

    o = lax.map(block, (qb, fq, jnp.arange(nq)))
    return jnp.moveaxis(o, 0, 1).reshape(b, S, H * Dh)


def fox_sample(q, k_new, v_new, logf_new, cache_k, cache_v, cache_logf, page_table):
    db, L, H, Dh = q.shape
    n_pages = page_table.shape[1]
    past = n_pages * PAGE_SIZE
    scale = Dh ** -0.5
    kp = cache_k[page_table].reshape(db, past, H, Dh)
    vp = cache_v[page_table].reshape(db, past, H, Dh)
    lfp = cache_logf[page_table].reshape(db, past, H).astype(F32)
    suffix = lax.cumsum(lfp, axis=1, reverse=True) - lfp
    fn = jnp.cumsum(logf_new.astype(F32), axis=1)
    fn_t = fn.transpose(0, 2, 1)
    s_past = jnp.einsum('bqhd,bkhd->bhqk', q, kp).astype(F32) * scale
    s_past = s_past + fn_t[..., :, None] + suffix.transpose(0, 2, 1)[:, :, None, :]
    s_new = jnp.einsum('bqhd,bkhd->bhqk', q, k_new).astype(F32) * scale
    s_new = s_new + fn_t[..., :, None] - fn_t[..., None, :]
    s_new = jnp.where(jnp.tril(jnp.ones((L, L), bool))[None, None], s_new, -jnp.inf)
    p = jax.nn.softmax(jnp.concatenate([s_past, s_new], axis=-1), axis=-1)
    o = jnp.einsum('bhqk,bkhd->bqhd', p[..., :past].astype(vp.dtype), vp)
    o = o + jnp.einsum('bhqk,bkhd->bqhd', p[..., past:].astype(v_new.dtype), v_new)
    return o.reshape(db, L, H * Dh)


def trunk(x, ssm_st, mconv_st, ffn_st, attend, norm_w, m_w_in, m_conv_w, m_conv_b, m_dt_bias, m_a_log,
          m_d, m_gnorm_w, m_w_out, ffn_w_in, ffn_conv_w, ffn_conv_b, ffn_w_out, kv_norm_w, w_kvf, b_f, w_q, w_o):
    bsz, L, _ = x.shape
    h = x
    new_ssm, new_mconv, new_ffn = [], [], []
    k = v = logf = None
    for i in range(DEPTH):
        if i < N_A:
            mix, c_new, s_new = mamba2_mixer(rmsnorm(h, norm_w[i, 0]), mconv_st[i], ssm_st[i], m_w_in[i],
                                             m_conv_w[i], m_conv_b[i], m_dt_bias[i], m_a_log[i], m_d[i],
                                             m_gnorm_w[i], m_w_out[i])
            new_ssm.append(s_new)
            new_mconv.append(c_new)
        else:
            j = i - N_A
            q = (rmsnorm(h, norm_w[i, 0]) @ w_q[j]).reshape(bsz, L, ATT_HEADS, ATT_HEAD_DIM)
            mix = attend(q, k, v, logf) @ w_o[j]
        h = h + rmsnorm(mix, norm_w[i, 1])
        f, f_buf = conv_ffn(rmsnorm(h, norm_w[i, 2]), ffn_st[i], ffn_w_in[i], ffn_conv_w[i], ffn_conv_b[i], ffn_w_out[i])
        new_ffn.append(f_buf)
        h = h + rmsnorm(f, norm_w[i, 3])
        if i == N_A - 1:
            k, v, logf = shared_kvf(h, kv_norm_w, w_kvf, b_f)
    return h, jnp.stack(new_ssm), jnp.stack(new_mconv), jnp.stack(new_ffn), k, v, logf


def setup_inputs(seed: int = 0) -> dict:
    key = jax.random.key(seed)
    ks = jax.random.split(key, 32)
    n_pages = PAST_LEN // PAGE_SIZE
    n_pool = (DEC_BATCH * n_pages * 5) // 4
    nrm = lambda k, shape, s: jax.random.normal(k, shape, F32) * s
    dt0 = jnp.exp(jax.random.uniform(ks[0], (N_A, SSM_HEADS), F32, math.log(1e-3), math.log(1e-1)))
    perm = jax.random.permutation(ks[1], n_pool)
    page_table = perm[:DEC_BATCH * n_pages].reshape(DEC_BATCH, n_pages).astype(jnp.int32)
    head_bias = jnp.linspace(FORGET_BIAS_LO, FORGET_BIAS_HI, ATT_HEADS, dtype=F32)
    return {
        'x_prompt': nrm(ks[2], (BATCH, SEQ, D_MODEL), 1.0),
        'x_sample': nrm(ks[3], (DEC_BATCH, DEC_SEQ, D_MODEL), 1.0),
        'state_ssm': nrm(ks[4], (N_A, DEC_BATCH, SSM_HEADS, SSM_HEAD_DIM, D_STATE), 0.5),
        'state_mconv': nrm(ks[5], (N_A, DEC_BATCH, CONV_W - 1, CONV_DIM), 1.0),
        'state_ffn': nrm(ks[6], (DEPTH, DEC_BATCH, FFN_CONV_W - 1, D_FF), 1.0),
        'cache_k': nrm(ks[7], (n_pool, PAGE_SIZE, ATT_HEADS, ATT_HEAD_DIM), 1.0),
        'cache_v': nrm(ks[8], (n_pool, PAGE_SIZE, ATT_HEADS, ATT_HEAD_DIM), 1.0),
        'cache_logf': jax.nn.log_sigmoid(head_bias + nrm(ks[9], (n_pool, PAGE_SIZE, ATT_HEADS), 1.0)),
        'page_table': page_table,
        'norm_w': 1.0 + nrm(ks[10], (DEPTH, 4, D_MODEL), 0.02),
        'm_w_in': nrm(ks[11], (N_A, D_MODEL, IN_PROJ_DIM), D_MODEL ** -0.5),
        'm_conv_w': nrm(ks[12], (N_A, CONV_W, CONV_DIM), CONV_W ** -0.5),
        'm_conv_b': nrm(ks[13], (N_A, CONV_DIM), 0.02),
        'm_dt_bias': dt0 + jnp.log(-jnp.expm1(-dt0)),
        'm_a_log': jnp.log(jax.random.uniform(ks[14], (N_A, SSM_HEADS), F32, 1.0, 16.0)),
        'm_d': 1.0 + nrm(ks[15], (N_A, SSM_HEADS), 0.1),
        'm_gnorm_w': 1.0 + nrm(ks[16], (N_A, D_INNER), 0.02),
        'm_w_out': nrm(ks[17], (N_A, D_INNER, D_MODEL), D_INNER ** -0.5),
        'ffn_w_in': nrm(ks[18], (DEPTH, D_MODEL, 2 * D_FF), D_MODEL ** -0.5),
        'ffn_conv_w': nrm(ks[19], (DEPTH, FFN_CONV_W, D_FF), FFN_CONV_W ** -0.5),
        'ffn_conv_b': nrm(ks[20], (DEPTH, D_FF), 0.02),
        'ffn_w_out': nrm(ks[21], (DEPTH, D_FF, D_MODEL), D_FF ** -0.5),
        'kv_norm_w': 1.0 + nrm(ks[22], (D_MODEL,), 0.02),
        'w_kvf': nrm(ks[23], (D_MODEL, 2 * ATT_DIM + ATT_HEADS), D_MODEL ** -0.5),
        'b_f': head_bias + nrm(ks[24], (ATT_HEADS,), 0.1),
        'w_q': nrm(ks[25], (N_B, D_MODEL, ATT_DIM), D_MODEL ** -0.5),
        'w_o': nrm(ks[26], (N_B, ATT_DIM, D_MODEL), ATT_DIM ** -0.5),
    }


def reference(x_prompt, x_sample, state_ssm, state_mconv, state_ffn, cache_k, cache_v, cache_logf, page_table,
              norm_w, m_w_in, m_conv_w, m_conv_b, m_dt_bias, m_a_log, m_d, m_gnorm_w, m_w_out,
              ffn_w_in, ffn_conv_w, ffn_conv_b, ffn_w_out, kv_norm_w, w_kvf, b_f, w_q, w_o):
    weights = (norm_w, m_w_in, m_conv_w, m_conv_b, m_dt_bias, m_a_log, m_d, m_gnorm_w, m_w_out,
               ffn_w_in, ffn_conv_w, ffn_conv_b, ffn_w_out, kv_norm_w, w_kvf, b_f, w_q, w_o)
    bp = x_prompt.shape[0]
    z_ssm = jnp.zeros((N_A, bp, SSM_HEADS, SSM_HEAD_DIM, D_STATE), F32)
    z_mconv = jnp.zeros((N_A, bp, CONV_W - 1, CONV_DIM), x_prompt.dtype)
    z_ffn = jnp.zeros((DEPTH, bp, FFN_CONV_W - 1, D_FF), x_prompt.dtype)
    y_prompt, p_ssm, p_mconv, p_ffn, p_k, p_v, p_logf = trunk(
        x_prompt, z_ssm, z_mconv, z_ffn, fox_prompt, *weights)
    attend_sample = lambda q, k, v, lf: fox_sample(q, k, v, lf, cache_k, cache_v, cache_logf, page_table)
    y_sample, s_ssm, s_mconv, s_ffn, s_k, s_v, s_logf = trunk(
        x_sample, state_ssm, state_mconv, state_ffn, attend_sample, *weights)
    return (y_prompt, y_sample, p_ssm, p_mconv, p_ffn, p_k, p_v, p_logf,
            s_ssm, s_mconv, s_ffn, s_k, s_v, s_logf)
```

```python
import functools
import math

import jax
import jax.numpy as jnp
from jax import lax
from jax.experimental import pallas as pl
from jax.experimental.pallas import tpu as pltpu

F32 = jnp.float32
BF16 = jnp.bfloat16
EPS = 1e-6

LANES = 128
SUBLANES = 8
VMEM_LIMIT_BYTES = 56 * 1024 * 1024

SSM_HEAD_DIM = 64
SSM_GROUPS = 4
D_STATE = 128
SSD_CHUNK = 128
ATT_HEAD_DIM = 64
PAGE_SIZE = 128


def _cparams(*sem):
    return pltpu.CompilerParams(dimension_semantics=sem, vmem_limit_bytes=VMEM_LIMIT_BYTES)


def _const_spec(shape):
    nd = len(shape)
    return pl.BlockSpec(shape, lambda *_: (0,) * nd)


def _rms(x, w):
    ms = jnp.mean(x * x, axis=-1, keepdims=True)
    return x * lax.rsqrt(ms + EPS) * w


def _silu(x):
    return x / (1.0 + jnp.exp(-x))


def _softplus(x):
    return jnp.maximum(x, 0.0) + jnp.log1p(jnp.exp(-jnp.abs(x)))


def _dot(a, b):
    return jnp.dot(a, b, preferred_element_type=F32)


def _dot_nt(a, b):
    return lax.dot_general(a, b, (((1,), (1,)), ((), ())), preferred_element_type=F32)


def _dot_tn(a, b):
    return lax.dot_general(a, b, (((0,), (0,)), ((), ())), preferred_element_type=F32)


def _split3(x):
    hi = x.astype(BF16)
    r1 = x - hi.astype(F32)
    mid = r1.astype(BF16)
    lo = (r1 - mid.astype(F32)).astype(BF16)
    return hi, mid, lo


def _dot3_x_sel(x, sel):
    hi, mid, lo = _split3(x)
    return _dot(hi, sel) + _dot(mid, sel) + _dot(lo, sel)


def _dot3_sel_x(sel, x):
    hi, mid, lo = _split3(x)
    return _dot(sel, hi) + _dot(sel, mid) + _dot(sel, lo)


def _delayed_rows(g, carry8, j):
    if j == 0:
        return g
    r = pltpu.roll(g, j, 0)
    c = pltpu.roll(carry8, j, 0)
    rows = lax.broadcasted_iota(jnp.int32, c.shape, 0)
    head = jnp.where(rows < j, c, r[:SUBLANES])
    return jnp.concatenate([head, r[SUBLANES:]], axis=0)


def _conv_rows(g, carry8, w, b):
    width = w.shape[0]
    out = b + _delayed_rows(g, carry8, width - 1) * w[0:1]
    for k in range(1, width):
        out = out + _delayed_rows(g, carry8, width - 1 - k) * w[k:k + 1]
    return out


def _conv_slabs(slabs, w, b, n_new):
    width = w.shape[0]
    outs = []
    for t in range(n_new):
        o = b + slabs[t] * w[0:1]
        for k in range(1, width):
            o = o + slabs[t + k] * w[k:k + 1]
        outs.append(o)
    return outs


def _col_chunks(n, step=512):
    out, c = [], 0
    while c < n:
        w = min(step, n - c)
        out.append((c, w))
        c += w
    return out


def _mamba_in_seq_kernel(x_ref, nw_ref, wz_ref, wx_ref, wdt_ref, cw_ref, cb_ref, dtb_ref,
                         z_ref, xa_ref, dt_ref, st_ref, xn_scr, carry_scr):
    li = pl.program_id(1)

    @pl.when(li == 0)
    def _():
        carry_scr[...] = jnp.zeros_like(carry_scr)

    xn_scr[...] = _rms(x_ref[0], nw_ref[...]).astype(BF16)
    xn = xn_scr[...]
    tm = xn.shape[0]
    for c0, cw in _col_chunks(wz_ref.shape[1]):
        z_ref[0, :, c0:c0 + cw] = _dot(xn, wz_ref[:, c0:c0 + cw])
    for c0, cw in _col_chunks(wx_ref.shape[1]):
        g = _dot(xn, wx_ref[:, c0:c0 + cw])
        carry8 = carry_scr[:, c0:c0 + cw]
        y = _conv_rows(g, carry8, cw_ref[:, c0:c0 + cw], cb_ref[:, c0:c0 + cw])
        xa_ref[0, :, c0:c0 + cw] = _silu(y)
        carry_scr[:, c0:c0 + cw] = g[tm - SUBLANES:]
    dt_ref[0] = _softplus(_dot(xn, wdt_ref[...]) + dtb_ref[...])
    st_ref[0] = carry_scr[...]


def _mamba_in_seq(h, nw, wz, wx, wdt, conv_w, conv_b, dtb, tm):
    bsz, seq, d = h.shape
    nz, nx = wz.shape[1], wx.shape[1]
    assert seq % tm == 0
    grid = (bsz, seq // tm)
    row = lambda n: pl.BlockSpec((1, tm, n), lambda b, l: (b, l, 0))
    return pl.pallas_call(
        _mamba_in_seq_kernel,
        grid=grid,
        in_specs=[row(d), _const_spec(nw.shape), _const_spec(wz.shape), _const_spec(wx.shape),
                  _const_spec(wdt.shape), _const_spec(conv_w.shape), _const_spec(conv_b.shape),
                  _const_spec(dtb.shape)],
        out_specs=[row(nz), row(nx), row(LANES),
                   pl.BlockSpec((1, SUBLANES, nx), lambda b, l: (b, 0, 0))],
        out_shape=[jax.ShapeDtypeStruct((bsz, seq, nz), F32),
                   jax.ShapeDtypeStruct((bsz, seq, nx), F32),
                   jax.ShapeDtypeStruct((bsz, seq, LANES), F32),
                   jax.ShapeDtypeStruct((bsz, SUBLANES, nx), F32)],
        scratch_shapes=[pltpu.VMEM((tm, d), BF16), pltpu.VMEM((SUBLANES, nx), F32)],
        compiler_params=_cparams("arbitrary", "arbitrary"),
        name="mamba_in_seq",
    )(h, nw, wz, wx, wdt, conv_w, conv_b, dtb)


def _mamba_in_step_kernel(n_new, x_ref, st_ref, nw_ref, wz_ref, wx_ref, wdt_ref, cw_ref, cb_ref, dtb_ref,
                          z_ref, xa_ref, dt_ref, nst_ref):
    xn = _rms(x_ref[...], nw_ref[...]).astype(BF16)
    nb = xn.shape[0] // n_new
    width = cw_ref.shape[0]
    for c0, cw in _col_chunks(wz_ref.shape[1]):
        z_ref[:, c0:c0 + cw] = _dot(xn, wz_ref[:, c0:c0 + cw])
    for c0, cw in _col_chunks(wx_ref.shape[1]):
        g = _dot(xn, wx_ref[:, c0:c0 + cw])
        slabs = [st_ref[k, :, c0:c0 + cw] for k in range(width - 1)]
        slabs += [g[t * nb:(t + 1) * nb] for t in range(n_new)]
        outs = _conv_slabs(slabs, cw_ref[:, c0:c0 + cw], cb_ref[:, c0:c0 + cw], n_new)
        for t in range(n_new):
            xa_ref[t * nb:(t + 1) * nb, c0:c0 + cw] = _silu(outs[t])
        for k in range(width - 1):
            nst_ref[k, :, c0:c0 + cw] = slabs[n_new + k]
    dt_ref[...] = _softplus(_dot(xn, wdt_ref[...]) + dtb_ref[...])


def _mamba_in_step(x_tm, st_tm, nw, wz, wx, wdt, conv_w, conv_b, dtb, n_new):
    m, d = x_tm.shape
    nz, nx = wz.shape[1], wx.shape[1]
    args = (x_tm, st_tm, nw, wz, wx, wdt, conv_w, conv_b, dtb)
    return pl.pallas_call(
        functools.partial(_mamba_in_step_kernel, n_new),
        grid=(1,),
        in_specs=[_const_spec(a.shape) for a in args],
        out_specs=[_const_spec((m, nz)), _const_spec((m, nx)), _const_spec((m, LANES)),
                   _const_spec(st_tm.shape)],
        out_shape=[jax.ShapeDtypeStruct((m, nz), F32), jax.ShapeDtypeStruct((m, nx), F32),
                   jax.ShapeDtypeStruct((m, LANES), F32), jax.ShapeDtypeStruct(st_tm.shape, F32)],
        compiler_params=_cparams("arbitrary"),
        name="mamba_in_step",
    )(*args)


def _ssd_kernel(valid, xa_ref, z_ref, dt_ref, s0_ref, alog_ref, dsk_ref, ex_ref, tri_ref,
                y_ref, sout_ref, st_scr):
    ci = pl.program_id(1)
    t_rows = xa_ref.shape[1]
    d_inner = z_ref.shape[2]
    gn = SSM_GROUPS * D_STATE
    n_heads = d_inner // SSM_HEAD_DIM
    heads_per_group = n_heads // SSM_GROUPS
    gcols = heads_per_group * SSM_HEAD_DIM

    @pl.when(ci == 0)
    def _():
        for c0 in range(0, d_inner, LANES):
            st_scr[:, c0:c0 + LANES] = s0_ref[0, c0:c0 + LANES, :].T

    xs = xa_ref[0, :, 0:d_inner]
    bm = xa_ref[0, :, d_inner:d_inner + gn].astype(BF16)
    cm = xa_ref[0, :, d_inner + gn:d_inner + 2 * gn].astype(BF16)
    dt = dt_ref[0]
    rows = lax.broadcasted_iota(jnp.int32, dt.shape, 0)
    if valid < t_rows:
        dt = jnp.where(rows < valid, dt, 0.0)
    dta = dt * -jnp.exp(alog_ref[...])
    acum = _dot3_sel_x(tri_ref[...], dta)
    acum_t = acum.T
    ex = ex_ref[...]
    dt_e = _dot3_x_sel(dt, ex)
    acum_e = _dot3_x_sel(acum, ex)
    xdt = xs * dt_e
    xdt_b = xdt.astype(BF16)
    last_e = acum_e[t_rows - 1:t_rows, :]
    e_acum = jnp.exp(acum_e)
    xde_b = (xdt * jnp.exp(last_e - acum_e)).astype(BF16)
    e_last = jnp.exp(last_e)

    li = lax.broadcasted_iota(jnp.int32, (t_rows, t_rows), 0)
    si = lax.broadcasted_iota(jnp.int32, (t_rows, t_rows), 1)
    causal = li >= si
    lane = lax.broadcasted_iota(jnp.int32, (t_rows, LANES), 1)
    first_head = lane < SSM_HEAD_DIM

    for g in range(SSM_GROUPS):
        c_g = cm[:, g * D_STATE:(g + 1) * D_STATE]
        b_g = bm[:, g * D_STATE:(g + 1) * D_STATE]
        cb = _dot_nt(c_g, b_g)
        g0 = g * gcols
        pieces = []
        for k in range(heads_per_group // 2):
            h1 = g * heads_per_group + 2 * k
            ms = []
            for hh in (h1, h1 + 1):
                seg = acum[:, hh:hh + 1] - acum_t[hh:hh + 1, :]
                dec = jnp.exp(jnp.where(causal, seg, -jnp.inf))
                ms.append((cb * dec).astype(BF16))
            x2 = xdt_b[:, g0 + k * LANES:g0 + (k + 1) * LANES]
            zero = jnp.zeros_like(x2)
            rhs = jnp.concatenate([jnp.where(first_head, x2, zero), jnp.where(first_head, zero, x2)], axis=0)
            pieces.append(_dot(jnp.concatenate(ms, axis=1), rhs))
        y_diag = jnp.concatenate(pieces, axis=1)
        st_g = st_scr[:, g0:g0 + gcols]
        y_off = _dot(c_g, st_g.astype(BF16)) * e_acum[:, g0:g0 + gcols]
        xs_g = xs[:, g0:g0 + gcols]
        y = y_diag + y_off + dsk_ref[:, g0:g0 + gcols] * xs_g
        y_ref[0, :, g0:g0 + gcols] = (y * _silu(z_ref[0, :, g0:g0 + gcols])).astype(y_ref.dtype)
        st_scr[:, g0:g0 + gcols] = e_last[:, g0:g0 + gcols] * st_g + _dot_tn(b_g, xde_b[:, g0:g0 + gcols])

    @pl.when(ci == pl.num_programs(1) - 1)
    def _():
        for c0 in range(0, d_inner, LANES):
            sout_ref[0, c0:c0 + LANES, :] = st_scr[:, c0:c0 + LANES].T


def _ssd(xa, z, dt, s0, a_row, dsk_e, ex, tri, valid):
    bsz, seq, nx = xa.shape
    d_inner = z.shape[2]
    t_rows = SSD_CHUNK
    assert seq % t_rows == 0
    row = lambda n: pl.BlockSpec((1, t_rows, n), lambda b, c: (b, c, 0))
    st_spec = pl.BlockSpec((1, d_inner, D_STATE), lambda b, c: (b, 0, 0))
    consts = (a_row, dsk_e, ex, tri)
    return pl.pallas_call(
        functools.partial(_ssd_kernel, valid),
        grid=(bsz, seq // t_rows),
        in_specs=[row(nx), row(d_inner), row(LANES), st_spec] + [_const_spec(c.shape) for c in consts],
        out_specs=[row(d_inner), st_spec],
        out_shape=[jax.ShapeDtypeStruct((bsz, seq, d_inner), F32),
                   jax.ShapeDtypeStruct((bsz, d_inner, D_STATE), F32)],
        scratch_shapes=[pltpu.VMEM((D_STATE, d_inner), F32)],
        compiler_params=_cparams("arbitrary", "arbitrary"),
        name="ssd_chunk",
    )(xa, z, dt, s0, *consts)


def _out_proj_kernel(pre_norm, a_ref, *refs):
    if pre_norm:
        gw_ref, w_ref, h_ref, pw_ref, o_ref = refs
        a = _rms(a_ref[...].astype(F32), gw_ref[...]).astype(BF16)
    else:
        w_ref, h_ref, pw_ref, o_ref = refs
        a = a_ref[...].astype(BF16)
    o_ref[...] = h_ref[...] + _rms(_dot(a, w_ref[...]), pw_ref[...])


def _out_proj(a, w, h, pw, tm, gw=None):
    m, k = a.shape
    d = w.shape[1]
    assert m % tm == 0
    row = lambda n: pl.BlockSpec((tm, n), lambda i: (i, 0))
    ins = [a] + ([gw] if gw is not None else []) + [w, h, pw]
    specs = [row(k)] + ([_const_spec(gw.shape)] if gw is not None else []) + \
        [_const_spec(w.shape), row(d), _const_spec(pw.shape)]
    return pl.pallas_call(
        functools.partial(_out_proj_kernel, gw is not None),
        grid=(m // tm,),
        in_specs=specs,
        out_specs=row(d),
        out_shape=jax.ShapeDtypeStruct((m, d), F32),
        compiler_params=_cparams("arbitrary"),
        name="out_proj",
    )(*ins)


def _ffn_in_seq_kernel(x_ref, nw_ref, wg_ref, wu_ref, cw_ref, cb_ref, act_ref, st_ref, xn_scr, carry_scr):
    li = pl.program_id(1)

    @pl.when(li == 0)
    def _():
        carry_scr[...] = jnp.zeros_like(carry_scr)

    xn_scr[...] = _rms(x_ref[0], nw_ref[...]).astype(BF16)
    xn = xn_scr[...]
    tm = xn.shape[0]
    for c0, cw in _col_chunks(wg_ref.shape[1]):
        g = _dot(xn, wg_ref[:, c0:c0 + cw])
        up = _dot(xn, wu_ref[:, c0:c0 + cw])
        y = _conv_rows(g, carry_scr[:, c0:c0 + cw], cw_ref[:, c0:c0 + cw], cb_ref[:, c0:c0 + cw])
        act_ref[0, :, c0:c0 + cw] = (_silu(y) * up).astype(act_ref.dtype)
        carry_scr[:, c0:c0 + cw] = g[tm - SUBLANES:]
    st_ref[0] = carry_scr[...]


def _ffn_in_seq(h, nw, wg, wu, conv_w, conv_b, tm):
    bsz, seq, d = h.shape
    nf = wg.shape[1]
    assert seq % tm == 0
    row = lambda n: pl.BlockSpec((1, tm, n), lambda b, l: (b, l, 0))
    return pl.pallas_call(
        _ffn_in_seq_kernel,
        grid=(bsz, seq // tm),
        in_specs=[row(d), _const_spec(nw.shape), _const_spec(wg.shape), _const_spec(wu.shape),
                  _const_spec(conv_w.shape), _const_spec(conv_b.shape)],
        out_specs=[row(nf), pl.BlockSpec((1, SUBLANES, nf), lambda b, l: (b, 0, 0))],
        out_shape=[jax.ShapeDtypeStruct((bsz, seq, nf), BF16),
                   jax.ShapeDtypeStruct((bsz, SUBLANES, nf), F32)],
        scratch_shapes=[pltpu.VMEM((tm, d), BF16), pltpu.VMEM((SUBLANES, nf), F32)],
        compiler_params=_cparams("arbitrary", "arbitrary"),
        name="ffn_in_seq",
    )(h, nw, wg, wu, conv_w, conv_b)


def _ffn_in_step_kernel(n_new, x_ref, st_ref, nw_ref, wg_ref, wu_ref, cw_ref, cb_ref, act_ref, nst_ref):
    xn = _rms(x_ref[...], nw_ref[...]).astype(BF16)
    nb = xn.shape[0] // n_new
    width = cw_ref.shape[0]
    for c0, cw in _col_chunks(wg_ref.shape[1]):
        g = _dot(xn, wg_ref[:, c0:c0 + cw])
        up = _dot(xn, wu_ref[:, c0:c0 + cw])
        slabs = [st_ref[k, :, c0:c0 + cw] for k in range(width - 1)]
        slabs += [g[t * nb:(t + 1) * nb] for t in range(n_new)]
        outs = _conv_slabs(slabs, cw_ref[:, c0:c0 + cw], cb_ref[:, c0:c0 + cw], n_new)
        for t in range(n_new):
            act_ref[t * nb:(t + 1) * nb, c0:c0 + cw] = \
                (_silu(outs[t]) * up[t * nb:(t + 1) * nb]).astype(act_ref.dtype)
        for k in range(width - 1):
            nst_ref[k, :, c0:c0 + cw] = slabs[n_new + k]


def _ffn_in_step(x_tm, st_tm, nw, wg, wu, conv_w, conv_b, n_new):
    m, d = x_tm.shape
    nf = wg.shape[1]
    args = (x_tm, st_tm, nw, wg, wu, conv_w, conv_b)
    return pl.pallas_call(
        functools.partial(_ffn_in_step_kernel, n_new),
        grid=(1,),
        in_specs=[_const_spec(a.shape) for a in args],
        out_specs=[_const_spec((m, nf)), _const_spec(st_tm.shape)],
        out_shape=[jax.ShapeDtypeStruct((m, nf), BF16), jax.ShapeDtypeStruct(st_tm.shape, F32)],
        compiler_params=_cparams("arbitrary"),
        name="ffn_in_step",
    )(*args)


def _kvf_kernel(x_ref, nw_ref, wk_ref, wv_ref, wf_ref, bf_ref, tri_ref,
                k_ref, v_ref, kb_ref, vb_ref, lf_ref, fc_ref, xn_scr, carry_scr):
    li = pl.program_id(1)

    @pl.when(li == 0)
    def _():
        carry_scr[...] = jnp.zeros_like(carry_scr)

    xn_scr[...] = _rms(x_ref[0], nw_ref[...]).astype(BF16)
    xn = xn_scr[...]
    for c0, cw in _col_chunks(wk_ref.shape[1]):
        k = _dot(xn, wk_ref[:, c0:c0 + cw])
        k_ref[0, :, c0:c0 + cw] = k
        kb_ref[0, :, c0:c0 + cw] = k.astype(BF16)
        v = _dot(xn, wv_ref[:, c0:c0 + cw])
        v_ref[0, :, c0:c0 + cw] = v
        vb_ref[0, :, c0:c0 + cw] = v.astype(BF16)
    f = _dot(xn, wf_ref[...]) + bf_ref[...]
    lf = -_softplus(-f)
    lf_ref[0] = lf
    fc = _dot3_sel_x(tri_ref[...], lf) + carry_scr[0:1, :]
    fc_ref[0] = fc
    tm = fc.shape[0]
    carry_scr[...] = jnp.broadcast_to(fc[tm - 1:tm, :], carry_scr.shape)


def _kvf(h, nw, wk, wv, wf, bf, tri, tm):
    bsz, seq, d = h.shape
    nk = wk.shape[1]
    assert seq % tm == 0
    row = lambda n: pl.BlockSpec((1, tm, n), lambda b, l: (b, l, 0))
    ins = (h, nw, wk, wv, wf, bf, tri)
    return pl.pallas_call(
        _kvf_kernel,
        grid=(bsz, seq // tm),
        in_specs=[row(d)] + [_const_spec(a.shape) for a in ins[1:]],
        out_specs=[row(nk), row(nk), row(nk), row(nk), row(LANES), row(LANES)],
        out_shape=[jax.ShapeDtypeStruct((bsz, seq, nk), F32), jax.ShapeDtypeStruct((bsz, seq, nk), F32),
                   jax.ShapeDtypeStruct((bsz, seq, nk), BF16), jax.ShapeDtypeStruct((bsz, seq, nk), BF16),
                   jax.ShapeDtypeStruct((bsz, seq, LANES), F32), jax.ShapeDtypeStruct((bsz, seq, LANES), F32)],
        scratch_shapes=[pltpu.VMEM((tm, d), BF16), pltpu.VMEM((SUBLANES, LANES), F32)],
        compiler_params=_cparams("arbitrary", "arbitrary"),
        name="kvf_proj",
    )(*ins)


def _q_proj_kernel(scale, x_ref, nw_ref, w_ref, q_ref):
    xn = _rms(x_ref[...], nw_ref[...]).astype(BF16)
    for c0, cw in _col_chunks(w_ref.shape[1]):
        q_ref[:, c0:c0 + cw] = (_dot(xn, w_ref[:, c0:c0 + cw]) * scale).astype(q_ref.dtype)


def _q_proj(h, nw, w, scale, tm):
    m, d = h.shape
    n = w.shape[1]
    assert m % tm == 0
    row = lambda c: pl.BlockSpec((tm, c), lambda i: (i, 0))
    return pl.pallas_call(
        functools.partial(_q_proj_kernel, scale),
        grid=(m // tm,),
        in_specs=[row(d), _const_spec(nw.shape), _const_spec(w.shape)],
        out_specs=row(n),
        out_shape=jax.ShapeDtypeStruct((m, n), BF16),
        compiler_params=_cparams("arbitrary"),
        name="q_proj",
    )(h, nw, w)


def _fox_prompt_kernel(q_ref, k_ref, v_ref, fq_ref, fk_ref, o_ref, m_scr, l_scr, acc_scr):
    hp = pl.program_id(1)
    qi = pl.program_id(2)
    tq = q_ref.shape[1]
    tk = tq
    q = q_ref[0]
    lane = lax.broadcasted_iota(jnp.int32, (tq, LANES), 1)
    first = lane < ATT_HEAD_DIM
    zq = jnp.zeros_like(q)
    qs = (jnp.where(first, q, zq), jnp.where(first, zq, q))
    fq_all = fq_ref[0]
    fqs = tuple(jnp.sum(jnp.where(lane == 2 * hp + i, fq_all, 0.0), axis=1, keepdims=True) for i in range(2))

    m_scr[...] = jnp.full_like(m_scr, -jnp.inf)
    l_scr[...] = jnp.zeros_like(l_scr)
    acc_scr[...] = jnp.zeros_like(acc_scr)

    def step(kj, masked):
        start = pl.multiple_of(kj * tk, tk)
        k = k_ref[0, pl.ds(start, tk), :]
        v = v_ref[0, pl.ds(start, tk), :]
        ss = []
        for i in range(2):
            fk = fk_ref[0, 2 * hp + i, pl.ds(kj, 1), :]
            s = _dot_nt(qs[i], k) + fqs[i] - fk
            if masked:
                r = lax.broadcasted_iota(jnp.int32, (tq, tk), 0)
                c = lax.broadcasted_iota(jnp.int32, (tq, tk), 1)
                s = jnp.where(r >= c, s, -jnp.inf)
            ss.append(s)
        m_old = m_scr[...]
        mx = [jnp.max(s, axis=1, keepdims=True) for s in ss]
        m_new = jnp.maximum(m_old, jnp.where(first, mx[0], mx[1]))
        alpha = jnp.exp(m_old - m_new)
        ps = [jnp.exp(ss[0] - m_new[:, 0:1]), jnp.exp(ss[1] - m_new[:, ATT_HEAD_DIM:ATT_HEAD_DIM + 1])]
        rs = [jnp.sum(p, axis=1, keepdims=True) for p in ps]
        l_scr[...] = alpha * l_scr[...] + jnp.where(first, rs[0], rs[1])
        zv = jnp.zeros_like(v)
        vcat = jnp.concatenate([jnp.where(first, v, zv), jnp.where(first, zv, v)], axis=0)
        pcat = jnp.concatenate([ps[0].astype(BF16), ps[1].astype(BF16)], axis=1)
        acc_scr[...] = alpha * acc_scr[...] + _dot(pcat, vcat)
        m_scr[...] = m_new

    def body(kj, c):
        step(kj, False)
        return c

    lax.fori_loop(0, qi, body, 0)
    step(qi, True)
    o_ref[0] = (acc_scr[...] / l_scr[...]).astype(o_ref.dtype)


def _fox_prompt(q, kb, vb, fc, fc_t, tq):
    bsz, seq, dm = q.shape
    n_pairs = dm // LANES
    assert seq % tq == 0
    nq = seq // tq
    n_heads = fc_t.shape[1]
    return pl.pallas_call(
        _fox_prompt_kernel,
        grid=(bsz, n_pairs, nq),
        in_specs=[pl.BlockSpec((1, tq, LANES), lambda b, p, i: (b, i, p)),
                  pl.BlockSpec((1, seq, LANES), lambda b, p, i: (b, 0, p)),
                  pl.BlockSpec((1, seq, LANES), lambda b, p, i: (b, 0, p)),
                  pl.BlockSpec((1, tq, LANES), lambda b, p, i: (b, i, 0)),
                  pl.BlockSpec((1, n_heads, nq, tq), lambda b, p, i: (b, 0, 0, 0))],
        out_specs=pl.BlockSpec((1, tq, LANES), lambda b, p, i: (b, i, p)),
        out_shape=jax.ShapeDtypeStruct((bsz, seq, dm), BF16),
        scratch_shapes=[pltpu.VMEM((tq, LANES), F32), pltpu.VMEM((tq, LANES), F32),
                        pltpu.VMEM((tq, LANES), F32)],
        compiler_params=_cparams("arbitrary", "arbitrary", "arbitrary"),
        name="fox_prompt",
    )(q, kb, vb, fc, fc_t)


def _fox_sample_kernel(n_group, n_new, pt_ref, q_ref, kn_ref, vn_ref, lfn_ref, suf_ref, *refs):
    k_refs = refs[0:n_group]
    v_refs = refs[n_group:2 * n_group]
    lf_refs = refs[2 * n_group:3 * n_group]
    o_ref, qb_scr, kpad_scr, vpad_scr, m_scr, l_scr, acc_scr, carry_scr = refs[3 * n_group:]
    i = pl.program_id(1)
    dm = q_ref.shape[2]
    n_heads = dm // ATT_HEAD_DIM
    n_rows = n_new * n_heads

    cols, run = [], None
    for t in range(n_new):
        c = lfn_ref[0, :, t:t + 1]
        run = c if run is None else run + c
        cols.append(run)
    fn_col = jnp.concatenate(cols, axis=0)

    @pl.when(i == 0)
    def _():
        hrow = lax.broadcasted_iota(jnp.int32, (n_heads, dm), 0)
        hcol = lax.broadcasted_iota(jnp.int32, (n_heads, dm), 1) // ATT_HEAD_DIM
        for t in range(n_new):
            qrow = jnp.broadcast_to(q_ref[0, t:t + 1, :].astype(F32), (n_heads, dm))
            qb_scr[t * n_heads:(t + 1) * n_heads, :] = jnp.where(hrow == hcol, qrow, 0.0).astype(BF16)
        kpad_scr[...] = jnp.zeros_like(kpad_scr)
        vpad_scr[...] = jnp.zeros_like(vpad_scr)
        kpad_scr[0:n_new, :] = kn_ref[0].astype(BF16)
        vpad_scr[0:n_new, :] = vn_ref[0].astype(BF16)
        s = _dot_nt(qb_scr[...], kpad_scr[...])
        key_lane = lax.broadcasted_iota(jnp.int32, (n_heads, LANES), 1)
        fn_keys = jnp.zeros((n_heads, LANES), F32)
        for u in range(n_new):
            fn_keys = jnp.where(key_lane == u, cols[u], fn_keys)
        fn_keys = jnp.concatenate([fn_keys] * n_new, axis=0)
        tok = lax.broadcasted_iota(jnp.int32, (n_rows, LANES), 0) // n_heads
        key = lax.broadcasted_iota(jnp.int32, (n_rows, LANES), 1)
        s = jnp.where(key <= tok, s + fn_col - fn_keys, -jnp.inf)
        m0 = jnp.max(s, axis=1, keepdims=True)
        p = jnp.exp(s - m0)
        m_scr[...] = jnp.broadcast_to(m0, m_scr.shape)
        l_scr[...] = jnp.broadcast_to(jnp.sum(p, axis=1, keepdims=True), l_scr.shape)
        acc_scr[...] = _dot(p.astype(BF16), vpad_scr[...])
        carry_scr[...] = jnp.zeros_like(carry_scr)

    qb = qb_scr[...]
    suf = suf_ref[...]
    ss = []
    carry = carry_scr[:, 0:1]
    for g in range(n_group):
        lf_t = lf_refs[g][0]
        within = _dot3_x_sel(lf_t, suf)
        bias = jnp.concatenate([within + carry] * n_new, axis=0) + fn_col
        ss.append(_dot_nt(qb, k_refs[g][0].astype(BF16)) + bias)
        carry = carry + jnp.sum(lf_t, axis=1, keepdims=True)
    carry_scr[...] = jnp.broadcast_to(carry, carry_scr.shape)

    m_old = m_scr[...]
    mx = ss[0]
    for s in ss[1:]:
        mx = jnp.maximum(mx, s)
    m_new = jnp.maximum(m_old, jnp.max(mx, axis=1, keepdims=True))
    alpha = jnp.exp(m_old - m_new)
    m_col = m_new[:, 0:1]
    acc = alpha[:, 0:1] * acc_scr[...]
    lsum = None
    for g in range(n_group):
        p = jnp.exp(ss[g] - m_col)
        r = jnp.sum(p, axis=1, keepdims=True)
        lsum = r if lsum is None else lsum + r
        acc = acc + _dot(p.astype(BF16), v_refs[g][0].astype(BF16))
    acc_scr[...] = acc
    l_scr[...] = alpha * l_scr[...] + lsum
    m_scr[...] = m_new

    @pl.when(i == pl.num_programs(1) - 1)
    def _():
        o = acc_scr[...] / l_scr[:, 0:1]
        hrow = lax.broadcasted_iota(jnp.int32, (n_heads, dm), 0)
        hcol = lax.broadcasted_iota(jnp.int32, (n_heads, dm), 1) // ATT_HEAD_DIM
        for t in range(n_new):
            blk = jnp.where(hrow == hcol, o[t * n_heads:(t + 1) * n_heads, :], 0.0)
            o_ref[0, t:t + 1, :] = jnp.sum(blk, axis=0, keepdims=True).astype(o_ref.dtype)


def _fox_sample(q, k_new, v_new, lfn_t, cache_k, cache_v, cache_lf_t, page_table, suf, n_group):
    db, n_new, dm = q.shape
    n_pages = page_table.shape[1]
    page = cache_k.shape[1]
    n_heads = cache_lf_t.shape[1]
    assert n_pages % n_group == 0 and page == LANES
    n_rows = n_new * n_heads
    pt_flat = page_table.reshape(-1)

    def page_map(g):
        def imap(b, i, pt):
            return (pt[b * n_pages + (n_pages - 1 - (i * n_group + g))], 0, 0)
        return imap

    seq_spec = lambda r, c: pl.BlockSpec((1, r, c), lambda b, i, pt: (b, 0, 0))
    in_specs = [seq_spec(n_new, dm), seq_spec(n_new, dm), seq_spec(n_new, dm), seq_spec(n_heads, LANES),
                pl.BlockSpec(suf.shape, lambda b, i, pt: (0, 0))]
    in_specs += [pl.BlockSpec((1, page, dm), page_map(g)) for g in range(n_group)]
    in_specs += [pl.BlockSpec((1, page, dm), page_map(g)) for g in range(n_group)]
    in_specs += [pl.BlockSpec((1, n_heads, page), page_map(g)) for g in range(n_group)]
    grid_spec = pltpu.PrefetchScalarGridSpec(
        num_scalar_prefetch=1,
        grid=(db, n_pages // n_group),
        in_specs=in_specs,
        out_specs=pl.BlockSpec((1, n_new, dm), lambda b, i, pt: (b, 0, 0)),
        scratch_shapes=[pltpu.VMEM((n_rows, dm), BF16), pltpu.VMEM((LANES, dm), BF16),
                        pltpu.VMEM((LANES, dm), BF16), pltpu.VMEM((n_rows, LANES), F32),
                        pltpu.VMEM((n_rows, LANES), F32), pltpu.VMEM((n_rows, dm), F32),
                        pltpu.VMEM((n_heads, LANES), F32)],
    )
    return pl.pallas_call(
        functools.partial(_fox_sample_kernel, n_group, n_new),
        grid_spec=grid_spec,
        out_shape=jax.ShapeDtypeStruct((db, n_new, dm), BF16),
        compiler_params=_cparams("arbitrary", "arbitrary"),
        name="fox_sample",
    )(pt_flat, q, k_new, v_new, lfn_t, suf, *([cache_k] * n_group), *([cache_v] * n_group),
      *([cache_lf_t] * n_group))


def _prep_weights(norm_w, m_w_in, m_conv_w, m_conv_b, m_dt_bias, m_a_log, m_d, m_gnorm_w, m_w_out,
                  ffn_w_in, ffn_conv_w, ffn_conv_b, ffn_w_out, kv_norm_w, w_kvf, b_f, w_q, w_o):
    d_model = norm_w.shape[-1]
    d_inner = m_w_out.shape[1]
    n_sheads = m_dt_bias.shape[1]
    conv_dim = m_conv_w.shape[2]
    d_ff = ffn_w_out.shape[1]
    att_dim = w_q.shape[2]
    n_aheads = b_f.shape[0]
    pad_lanes = lambda a: jnp.pad(a, [(0, 0)] * (a.ndim - 1) + [(0, LANES - a.shape[-1])])
    w = {}
    w["norm"] = norm_w.reshape(norm_w.shape[0], norm_w.shape[1], 1, d_model)
    w["m_wz"] = m_w_in[:, :, :d_inner].astype(BF16)
    w["m_wx"] = m_w_in[:, :, d_inner:d_inner + conv_dim].astype(BF16)
    w["m_wdt"] = pad_lanes(m_w_in[:, :, d_inner + conv_dim:]).astype(BF16)
    w["m_conv_w"] = m_conv_w
    w["m_conv_b"] = m_conv_b[:, None, :]
    w["m_dtb"] = pad_lanes(m_dt_bias)[:, None, :]
    w["m_alog"] = pad_lanes(m_a_log)[:, None, :]
    w["m_dsk"] = jnp.repeat(m_d.astype(F32), SSM_HEAD_DIM, axis=1)[:, None, :]
    w["m_gw"] = m_gnorm_w[:, None, :]
    w["m_wout"] = m_w_out.astype(BF16)
    w["f_wg"] = ffn_w_in[:, :, :d_ff].astype(BF16)
    w["f_wu"] = ffn_w_in[:, :, d_ff:].astype(BF16)
    w["f_conv_w"] = ffn_conv_w
    w["f_conv_b"] = ffn_conv_b[:, None, :]
    w["f_wout"] = ffn_w_out.astype(BF16)
    w["kv_norm"] = kv_norm_w[None, :]
    w["wk"] = w_kvf[:, :att_dim].astype(BF16)
    w["wv"] = w_kvf[:, att_dim:2 * att_dim].astype(BF16)
    w["wf"] = pad_lanes(w_kvf[:, 2 * att_dim:]).astype(BF16)
    w["bf"] = pad_lanes(b_f[None, :])
    w["wq"] = w_q.astype(BF16)
    w["wo"] = w_o.astype(BF16)
    hh = jnp.arange(LANES)[:, None]
    cc = jnp.arange(d_inner)[None, :] // SSM_HEAD_DIM
    w["ex"] = (hh == cc).astype(BF16)
    r = jnp.arange(SSD_CHUNK)
    w["tri_chunk"] = (r[:, None] >= r[None, :]).astype(BF16)
    p = jnp.arange(PAGE_SIZE)
    w["suf"] = (p[:, None] > p[None, :]).astype(BF16)
    w["n_sheads"] = n_sheads
    w["n_aheads"] = n_aheads
    return w


TM_SEQ = 256
TQ = 512
N_PAGE_GROUP = 8


def _tri(n):
    r = jnp.arange(n)
    return (r[:, None] >= r[None, :]).astype(BF16)


def _prompt_trunk(x, w, depth, n_a):
    bsz, seq, d = x.shape
    m = bsz * seq
    h = x
    d_inner = w["m_wout"].shape[1]
    outs_ssm, outs_mconv, outs_ffn = [], [], []
    kf = vf = kb = vb = lf = fc = None
    for i in range(depth):
        nw = w["norm"][i]
        if i < n_a:
            z, xa, dt, cst = _mamba_in_seq(h, nw[0], w["m_wz"][i], w["m_wx"][i], w["m_wdt"][i],
                                           w["m_conv_w"][i], w["m_conv_b"][i], w["m_dtb"][i], TM_SEQ)
            s0 = jnp.zeros((bsz, d_inner, D_STATE), F32)
            y, s_new = _ssd(xa, z, dt, s0, w["m_alog"][i], w["m_dsk"][i], w["ex"], w["tri_chunk"], SSD_CHUNK)
            h = _out_proj(y.reshape(m, d_inner), w["m_wout"][i], h.reshape(m, d), nw[1], TM_SEQ,
                          gw=w["m_gw"][i]).reshape(bsz, seq, d)
            width = w["m_conv_w"].shape[1]
            outs_mconv.append(cst[:, SUBLANES - (width - 1):, :])
            outs_ssm.append(s_new.reshape(bsz, w["n_sheads"], SSM_HEAD_DIM, D_STATE))
        else:
            j = i - n_a
            q = _q_proj(h.reshape(m, d), nw[0], w["wq"][j], ATT_HEAD_DIM ** -0.5, TM_SEQ).reshape(bsz, seq, -1)
            fc_t = jnp.swapaxes(fc[:, :, :w["n_aheads"]], 1, 2).reshape(bsz, w["n_aheads"], seq // TQ, TQ)
            o = _fox_prompt(q, kb, vb, fc, fc_t, TQ)
            h = _out_proj(o.reshape(m, -1), w["wo"][j], h.reshape(m, d), nw[1], TM_SEQ).reshape(bsz, seq, d)
        act, fst = _ffn_in_seq(h, nw[2], w["f_wg"][i], w["f_wu"][i], w["f_conv_w"][i], w["f_conv_b"][i], TM_SEQ)
        h = _out_proj(act.reshape(m, -1), w["f_wout"][i], h.reshape(m, d), nw[3], TM_SEQ).reshape(bsz, seq, d)
        fwidth = w["f_conv_w"].shape[1]
        outs_ffn.append(fst[:, SUBLANES - (fwidth - 1):, :])
        if i == n_a - 1:
            kf, vf, kb, vb, lf, fc = _kvf(h, w["kv_norm"], w["wk"], w["wv"], w["wf"], w["bf"], _tri(TM_SEQ), TM_SEQ)
    n_ah = w["n_aheads"]
    return (h, jnp.stack(outs_ssm), jnp.stack(outs_mconv), jnp.stack(outs_ffn),
            kf.reshape(bsz, seq, n_ah, ATT_HEAD_DIM), vf.reshape(bsz, seq, n_ah, ATT_HEAD_DIM), lf[:, :, :n_ah])


def _to_tm(a):
    db, n_new, c = a.shape
    return jnp.swapaxes(a, 0, 1).reshape(n_new * db, c)


def _from_tm(a, db):
    m, c = a.shape
    return jnp.swapaxes(a.reshape(m // db, db, c), 0, 1)


def _sample_trunk(x, ssm_st, mconv_st, ffn_st, cache_k, cache_v, cache_logf, page_table, w, depth, n_a):
    db, n_new, d = x.shape
    m = db * n_new
    h = _to_tm(x)
    d_inner = w["m_wout"].shape[1]
    n_ah = w["n_aheads"]
    outs_ssm, outs_mconv, outs_ffn = [], [], []
    kf = vf = lf = None
    pad_t = lambda a: jnp.pad(a, ((0, 0), (0, SSD_CHUNK - n_new), (0, 0)))
    for i in range(depth):
        nw = w["norm"][i]
        if i < n_a:
            st_tm = jnp.swapaxes(mconv_st[i], 0, 1)
            z, xa, dt, nst = _mamba_in_step(h, st_tm, nw[0], w["m_wz"][i], w["m_wx"][i], w["m_wdt"][i],
                                            w["m_conv_w"][i], w["m_conv_b"][i], w["m_dtb"][i], n_new)
            s0 = ssm_st[i].reshape(db, d_inner, D_STATE)
            y, s_new = _ssd(pad_t(_from_tm(xa, db)), pad_t(_from_tm(z, db)), pad_t(_from_tm(dt, db)), s0,
                            w["m_alog"][i], w["m_dsk"][i], w["ex"], w["tri_chunk"], n_new)
            y_tm = _to_tm(y[:, :n_new, :])
            h = _out_proj(y_tm, w["m_wout"][i], h, nw[1], m, gw=w["m_gw"][i])
            outs_mconv.append(jnp.swapaxes(nst, 0, 1))
            outs_ssm.append(s_new.reshape(db, w["n_sheads"], SSM_HEAD_DIM, D_STATE))
        else:
            j = i - n_a
            q = _from_tm(_q_proj(h, nw[0], w["wq"][j], ATT_HEAD_DIM ** -0.5, m), db)
            pool = cache_k.shape[0]
            ck = cache_k.reshape(pool, PAGE_SIZE, -1)
            cv = cache_v.reshape(pool, PAGE_SIZE, -1)
            clf_t = jnp.swapaxes(cache_logf, 1, 2)
            lfn_t = jnp.pad(jnp.swapaxes(lf[:, :, :n_ah], 1, 2), ((0, 0), (0, 0), (0, LANES - n_new)))
            o = _fox_sample(q, kf, vf, lfn_t, ck, cv, clf_t, page_table, w["suf"], N_PAGE_GROUP)
            h = _out_proj(_to_tm(o), w["wo"][j], h, nw[1], m)
        fst_tm = jnp.swapaxes(ffn_st[i], 0, 1)
        act, nfst = _ffn_in_step(h, fst_tm, nw[2], w["f_wg"][i], w["f_wu"][i], w["f_conv_w"][i],
                                 w["f_conv_b"][i], n_new)
        h = _out_proj(act, w["f_wout"][i], h, nw[3], m)
        outs_ffn.append(jnp.swapaxes(nfst, 0, 1))
        if i == n_a - 1:
            kf_tm, vf_tm, _, _, lf_tm, _ = _kvf(h[None], w["kv_norm"], w["wk"], w["wv"], w["wf"], w["bf"],
                                                _tri(m), m)
            kf, vf, lf = _from_tm(kf_tm[0], db), _from_tm(vf_tm[0], db), _from_tm(lf_tm[0], db)
    return (_from_tm(h, db), jnp.stack(outs_ssm), jnp.stack(outs_mconv), jnp.stack(outs_ffn),
            kf.reshape(db, n_new, n_ah, ATT_HEAD_DIM), vf.reshape(db, n_new, n_ah, ATT_HEAD_DIM), lf[:, :, :n_ah])


def kernel(x_prompt, x_sample, state_ssm, state_mconv, state_ffn, cache_k, cache_v, cache_logf, page_table,
           norm_w, m_w_in, m_conv_w, m_conv_b, m_dt_bias, m_a_log, m_d, m_gnorm_w, m_w_out,
           ffn_w_in, ffn_conv_w, ffn_conv_b, ffn_w_out, kv_norm_w, w_kvf, b_f, w_q, w_o):
    depth = norm_w.shape[0]
    n_a = m_w_in.shape[0]
    w = _prep_weights(norm_w, m_w_in, m_conv_w, m_conv_b, m_dt_bias, m_a_log, m_d, m_gnorm_w, m_w_out,
                      ffn_w_in, ffn_conv_w, ffn_conv_b, ffn_w_out, kv_norm_w, w_kvf, b_f, w_q, w_o)
    y_p, p_ssm, p_mconv, p_ffn, p_k, p_v, p_lf = _prompt_trunk(x_prompt, w, depth, n_a)
    y_s, s_ssm, s_mconv, s_ffn, s_k, s_v, s_lf = _sample_trunk(
        x_sample, state_ssm, state_mconv, state_ffn, cache_k, cache_v, cache_logf, page_table, w, depth, n_a)
    return (y_p, y_s, p_ssm, p_mconv, p_ffn, p_k, p_v, p_lf, s_ssm, s_mconv, s_ffn, s_k, s_v, s_lf)
```

```python
import functools
import math

import jax
import jax.numpy as jnp
from jax import lax
from jax.experimental import pallas as pl
from jax.experimental.pallas import tpu as pltpu

F32 = jnp.float32
BF16 = jnp.bfloat16
EPS = 1e-6

LANES = 128
SUBLANES = 8
VMEM_LIMIT_BYTES = 56 * 1024 * 1024

SSM_HEAD_DIM = 64
SSM_GROUPS = 4
D_STATE = 128
SSD_CHUNK = 128
ATT_HEAD_DIM = 64
PAGE_SIZE = 128


def _cparams(*sem):
    return pltpu.CompilerParams(dimension_semantics=sem, vmem_limit_bytes=VMEM_LIMIT_BYTES)


def _const_spec(shape):
    nd = len(shape)
    return pl.BlockSpec(shape, lambda *_: (0,) * nd)


def _rms(x, w):
    ms = jnp.mean(x * x, axis=-1, keepdims=True)
    return x * lax.rsqrt(ms + EPS) * w


def _silu(x):
    return x / (1.0 + jnp.exp(-x))


def _softplus(x):
    return jnp.maximum(x, 0.0) + jnp.log1p(jnp.exp(-jnp.abs(x)))


def _dot(a, b):
    return jnp.dot(a, b, preferred_element_type=F32)


def _dot_nt(a, b):
    return lax.dot_general(a, b, (((1,), (1,)), ((), ())), preferred_element_type=F32)


def _dot_tn(a, b):
    return lax.dot_general(a, b, (((0,), (0,)), ((), ())), preferred_element_type=F32)


def _split3(x):
    hi = x.astype(BF16)
    r1 = x - hi.astype(F32)
    mid = r1.astype(BF16)
    lo = (r1 - mid.astype(F32)).astype(BF16)
    return hi, mid, lo


def _dot3_x_sel(x, sel):
    hi, mid, lo = _split3(x)
    return _dot(hi, sel) + _dot(mid, sel) + _dot(lo, sel)


def _dot3_sel_x(sel, x):
    hi, mid, lo = _split3(x)
    return _dot(sel, hi) + _dot(sel, mid) + _dot(sel, lo)


def _delayed_rows(g, carry8, j):
    if j == 0:
        return g
    r = pltpu.roll(g, j, 0)
    c = pltpu.roll(carry8, j, 0)
    rows = lax.broadcasted_iota(jnp.int32, c.shape, 0)
    head = jnp.where(rows < j, c, r[:SUBLANES])
    return jnp.concatenate([head, r[SUBLANES:]], axis=0)


def _conv_rows(g, carry8, w, b):
    width = w.shape[0]
    out = b + _delayed_rows(g, carry8, width - 1) * w[0:1]
    for k in range(1, width):
        out = out + _delayed_rows(g, carry8, width - 1 - k) * w[k:k + 1]
    return out


def _conv_slabs(slabs, w, b, n_new):
    width = w.shape[0]
    outs = []
    for t in range(n_new):
        o = b + slabs[t] * w[0:1]
        for k in range(1, width):
            o = o + slabs[t + k] * w[k:k + 1]
        outs.append(o)
    return outs


def _col_chunks(n, step=512):
    out, c = [], 0
    while c < n:
        w = min(step, n - c)
        out.append((c, w))
        c += w
    return out


def _mamba_in_seq_kernel(x_ref, nw_ref, wz_ref, wx_ref, wdt_ref, cw_ref, cb_ref, dtb_ref,
                         z_ref, xa_ref, dt_ref, st_ref, xn_scr, carry_scr):
    li = pl.program_id(1)

    @pl.when(li == 0)
    def _():
        carry_scr[...] = jnp.zeros_like(carry_scr)

    xn_scr[...] = _rms(x_ref[0], nw_ref[...]).astype(BF16)
    xn = xn_scr[...]
    tm = xn.shape[0]
    for c0, cw in _col_chunks(wz_ref.shape[1]):
        z_ref[0, :, c0:c0 + cw] = _dot(xn, wz_ref[:, c0:c0 + cw])
    for c0, cw in _col_chunks(wx_ref.shape[1]):
        g = _dot(xn, wx_ref[:, c0:c0 + cw])
        carry8 = carry_scr[:, c0:c0 + cw]
        y = _conv_rows(g, carry8, cw_ref[:, c0:c0 + cw], cb_ref[:, c0:c0 + cw])
        xa_ref[0, :, c0:c0 + cw] = _silu(y)
        carry_scr[:, c0:c0 + cw] = g[tm - SUBLANES:]
    dt_ref[0] = _softplus(_dot(xn, wdt_ref[...]) + dtb_ref[...])
    st_ref[0] = carry_scr[...]


def _mamba_in_seq(h, nw, wz, wx, wdt, conv_w, conv_b, dtb, tm):
    bsz, seq, d = h.shape
    nz, nx = wz.shape[1], wx.shape[1]
    assert seq % tm == 0
    grid = (bsz, seq // tm)
    row = lambda n: pl.BlockSpec((1, tm, n), lambda b, l: (b, l, 0))
    return pl.pallas_call(
        _mamba_in_seq_kernel,
        grid=grid,
        in_specs=[row(d), _const_spec(nw.shape), _const_spec(wz.shape), _const_spec(wx.shape),
                  _const_spec(wdt.shape), _const_spec(conv_w.shape), _const_spec(conv_b.shape),
                  _const_spec(dtb.shape)],
        out_specs=[row(nz), row(nx), row(LANES),
                   pl.BlockSpec((1, SUBLANES, nx), lambda b, l: (b, 0, 0))],
        out_shape=[jax.ShapeDtypeStruct((bsz, seq, nz), F32),
                   jax.ShapeDtypeStruct((bsz, seq, nx), F32),
                   jax.ShapeDtypeStruct((bsz, seq, LANES), F32),
                   jax.ShapeDtypeStruct((bsz, SUBLANES, nx), F32)],
        scratch_shapes=[pltpu.VMEM((tm, d), BF16), pltpu.VMEM((SUBLANES, nx), F32)],
        compiler_params=_cparams("arbitrary", "arbitrary"),
        name="mamba_in_seq",
    )(h, nw, wz, wx, wdt, conv_w, conv_b, dtb)


def _mamba_in_step_kernel(n_new, x_ref, st_ref, nw_ref, wz_ref, wx_ref, wdt_ref, cw_ref, cb_ref, dtb_ref,
                          z_ref, xa_ref, dt_ref, nst_ref):
    xn = _rms(x_ref[...], nw_ref[...]).astype(BF16)
    nb = xn.shape[0] // n_new
    width = cw_ref.shape[0]
    for c0, cw in _col_chunks(wz_ref.shape[1]):
        z_ref[:, c0:c0 + cw] = _dot(xn, wz_ref[:, c0:c0 + cw])
    for c0, cw in _col_chunks(wx_ref.shape[1]):
        g = _dot(xn, wx_ref[:, c0:c0 + cw])
        slabs = [st_ref[k, :, c0:c0 + cw] for k in range(width - 1)]
        slabs += [g[t * nb:(t + 1) * nb] for t in range(n_new)]
        outs = _conv_slabs(slabs, cw_ref[:, c0:c0 + cw], cb_ref[:, c0:c0 + cw], n_new)
        for t in range(n_new):
            xa_ref[t * nb:(t + 1) * nb, c0:c0 + cw] = _silu(outs[t])
        for k in range(width - 1):
            nst_ref[k, :, c0:c0 + cw] = slabs[n_new + k]
    dt_ref[...] = _softplus(_dot(xn, wdt_ref[...]) + dtb_ref[...])


def _mamba_in_step(x_tm, st_tm, nw, wz, wx, wdt, conv_w, conv_b, dtb, n_new):
    m, d = x_tm.shape
    nz, nx = wz.shape[1], wx.shape[1]
    args = (x_tm, st_tm, nw, wz, wx, wdt, conv_w, conv_b, dtb)
    return pl.pallas_call(
        functools.partial(_mamba_in_step_kernel, n_new),
        grid=(1,),
        in_specs=[_const_spec(a.shape) for a in args],
        out_specs=[_const_spec((m, nz)), _const_spec((m, nx)), _const_spec((m, LANES)),
                   _const_spec(st_tm.shape)],
        out_shape=[jax.ShapeDtypeStruct((m, nz), F32), jax.ShapeDtypeStruct((m, nx), F32),
                   jax.ShapeDtypeStruct((m, LANES), F32), jax.ShapeDtypeStruct(st_tm.shape, F32)],
        compiler_params=_cparams("arbitrary"),
        name="mamba_in_step",
    )(*args)


def _ssd_kernel(valid, xa_ref, z_ref, dt_ref, s0_ref, alog_ref, dsk_ref, ex_ref, tri_ref,
                y_ref, sout_ref, st_scr):
    ci = pl.program_id(1)
    t_rows = xa_ref.shape[1]
    d_inner = z_ref.shape[2]
    gn = SSM_GROUPS * D_STATE
    n_heads = d_inner // SSM_HEAD_DIM
    heads_per_group = n_heads // SSM_GROUPS
    gcols = heads_per_group * SSM_HEAD_DIM

    @pl.when(ci == 0)
    def _():
        for c0 in range(0, d_inner, LANES):
            st_scr[:, c0:c0 + LANES] = s0_ref[0, c0:c0 + LANES, :].T

    xs = xa_ref[0, :, 0:d_inner]
    bm = xa_ref[0, :, d_inner:d_inner + gn].astype(BF16)
    cm = xa_ref[0, :, d_inner + gn:d_inner + 2 * gn].astype(BF16)
    dt = dt_ref[0]
    rows = lax.broadcasted_iota(jnp.int32, dt.shape, 0)
    if valid < t_rows:
        dt = jnp.where(rows < valid, dt, 0.0)
    dta = dt * -jnp.exp(alog_ref[...])
    acum = _dot3_sel_x(tri_ref[...], dta)
    acum_t = acum.T
    ex = ex_ref[...]
    dt_e = _dot3_x_sel(dt, ex)
    acum_e = _dot3_x_sel(acum, ex)
    xdt = xs * dt_e
    xdt_b = xdt.astype(BF16)
    last_e = acum_e[t_rows - 1:t_rows, :]
    e_acum = jnp.exp(acum_e)
    xde_b = (xdt * jnp.exp(last_e - acum_e)).astype(BF16)
    e_last = jnp.exp(last_e)

    li = lax.broadcasted_iota(jnp.int32, (t_rows, t_rows), 0)
    si = lax.broadcasted_iota(jnp.int32, (t_rows, t_rows), 1)
    causal = li >= si
    lane = lax.broadcasted_iota(jnp.int32, (t_rows, LANES), 1)
    first_head = lane < SSM_HEAD_DIM

    for g in range(SSM_GROUPS):
        c_g = cm[:, g * D_STATE:(g + 1) * D_STATE]
        b_g = bm[:, g * D_STATE:(g + 1) * D_STATE]
        cb = _dot_nt(c_g, b_g)
        g0 = g * gcols
        pieces = []
        for k in range(heads_per_group // 2):
            h1 = g * heads_per_group + 2 * k
            ms = []
            for hh in (h1, h1 + 1):
                seg = acum[:, hh:hh + 1] - acum_t[hh:hh + 1, :]
                dec = jnp.exp(jnp.where(causal, seg, -jnp.inf))
                ms.append((cb * dec).astype(BF16))
            x2 = xdt_b[:, g0 + k * LANES:g0 + (k + 1) * LANES]
            zero = jnp.zeros_like(x2)
            rhs = jnp.concatenate([jnp.where(first_head, x2, zero), jnp.where(first_head, zero, x2)], axis=0)
            pieces.append(_dot(jnp.concatenate(ms, axis=1), rhs))
        y_diag = jnp.concatenate(pieces, axis=1)
        st_g = st_scr[:, g0:g0 + gcols]
        y_off = _dot(c_g, st_g.astype(BF16)) * e_acum[:, g0:g0 + gcols]
        xs_g = xs[:, g0:g0 + gcols]
        y = y_diag + y_off + dsk_ref[:, g0:g0 + gcols] * xs_g
        y_ref[0, :, g0:g0 + gcols] = (y * _silu(z_ref[0, :, g0:g0 + gcols])).astype(y_ref.dtype)
        st_scr[:, g0:g0 + gcols] = e_last[:, g0:g0 + gcols] * st_g + _dot_tn(b_g, xde_b[:, g0:g0 + gcols])

    @pl.when(ci == pl.num_programs(1) - 1)
    def _():
        for c0 in range(0, d_inner, LANES):
            sout_ref[0, c0:c0 + LANES, :] = st_scr[:, c0:c0 + LANES].T


def _ssd(xa, z, dt, s0, a_row, dsk_e, ex, tri, valid):
    bsz, seq, nx = xa.shape
    d_inner = z.shape[2]
    t_rows = SSD_CHUNK
    assert seq % t_rows == 0
    row = lambda n: pl.BlockSpec((1, t_rows, n), lambda b, c: (b, c, 0))
    st_spec = pl.BlockSpec((1, d_inner, D_STATE), lambda b, c: (b, 0, 0))
    consts = (a_row, dsk_e, ex, tri)
    return pl.pallas_call(
        functools.partial(_ssd_kernel, valid),
        grid=(bsz, seq // t_rows),
        in_specs=[row(nx), row(d_inner), row(LANES), st_spec] + [_const_spec(c.shape) for c in consts],
        out_specs=[row(d_inner), st_spec],
        out_shape=[jax.ShapeDtypeStruct((bsz, seq, d_inner), F32),
                   jax.ShapeDtypeStruct((bsz, d_inner, D_STATE), F32)],
        scratch_shapes=[pltpu.VMEM((D_STATE, d_inner), F32)],
        compiler_params=_cparams("arbitrary", "arbitrary"),
        name="ssd_chunk",
    )(xa, z, dt, s0, *consts)


def _out_proj_kernel(pre_norm, a_ref, *refs):
    if pre_norm:
        gw_ref, w_ref, h_ref, pw_ref, o_ref = refs
        a = _rms(a_ref[...].astype(F32), gw_ref[...]).astype(BF16)
    else:
        w_ref, h_ref, pw_ref, o_ref = refs
        a = a_ref[...].astype(BF16)
    o_ref[...] = h_ref[...] + _rms(_dot(a, w_ref[...]), pw_ref[...])


def _out_proj(a, w, h, pw, tm, gw=None):
    m, k = a.shape
    d = w.shape[1]
    assert m % tm == 0
    row = lambda n: pl.BlockSpec((tm, n), lambda i: (i, 0))
    ins = [a] + ([gw] if gw is not None else []) + [w, h, pw]
    specs = [row(k)] + ([_const_spec(gw.shape)] if gw is not None else []) + \
        [_const_spec(w.shape), row(d), _const_spec(pw.shape)]
    return pl.pallas_call(
        functools.partial(_out_proj_kernel, gw is not None),
        grid=(m // tm,),
        in_specs=specs,
        out_specs=row(d),
        out_shape=jax.ShapeDtypeStruct((m, d), F32),
        compiler_params=_cparams("arbitrary"),
        name="out_proj",
    )(*ins)


def _ffn_in_seq_kernel(x_ref, nw_ref, wg_ref, wu_ref, cw_ref, cb_ref, act_ref, st_ref, xn_scr, carry_scr):
    li = pl.program_id(1)

    @pl.when(li == 0)
    def _():
        carry_scr[...] = jnp.zeros_like(carry_scr)

    xn_scr[...] = _rms(x_ref[0], nw_ref[...]).astype(BF16)
    xn = xn_scr[...]
    tm = xn.shape[0]
    for c0, cw in _col_chunks(wg_ref.shape[1]):
        g = _dot(xn, wg_ref[:, c0:c0 + cw])
        up = _dot(xn, wu_ref[:, c0:c0 + cw])
        y = _conv_rows(g, carry_scr[:, c0:c0 + cw], cw_ref[:, c0:c0 + cw], cb_ref[:, c0:c0 + cw])
        act_ref[0, :, c0:c0 + cw] = (_silu(y) * up).astype(act_ref.dtype)
        carry_scr[:, c0:c0 + cw] = g[tm - SUBLANES:]
    st_ref[0] = carry_scr[...]


def _ffn_in_seq(h, nw, wg, wu, conv_w, conv_b, tm):
    bsz, seq, d = h.shape
    nf = wg.shape[1]
    assert seq % tm == 0
    row = lambda n: pl.BlockSpec((1, tm, n), lambda b, l: (b, l, 0))
    return pl.pallas_call(
        _ffn_in_seq_kernel,
        grid=(bsz, seq // tm),
        in_specs=[row(d), _const_spec(nw.shape), _const_spec(wg.shape), _const_spec(wu.shape),
                  _const_spec(conv_w.shape), _const_spec(conv_b.shape)],
        out_specs=[row(nf), pl.BlockSpec((1, SUBLANES, nf), lambda b, l: (b, 0, 0))],
        out_shape=[jax.ShapeDtypeStruct((bsz, seq, nf), BF16),
                   jax.ShapeDtypeStruct((bsz, SUBLANES, nf), F32)],
        scratch_shapes=[pltpu.VMEM((tm, d), BF16), pltpu.VMEM((SUBLANES, nf), F32)],
        compiler_params=_cparams("arbitrary", "arbitrary"),
        name="ffn_in_seq",
    )(h, nw, wg, wu, conv_w, conv_b)


def _ffn_in_step_kernel(n_new, x_ref, st_ref, nw_ref, wg_ref, wu_ref, cw_ref, cb_ref, act_ref, nst_ref):
    xn = _rms(x_ref[...], nw_ref[...]).astype(BF16)
    nb = xn.shape[0] // n_new
    width = cw_ref.shape[0]
    for c0, cw in _col_chunks(wg_ref.shape[1]):
        g = _dot(xn, wg_ref[:, c0:c0 + cw])
        up = _dot(xn, wu_ref[:, c0:c0 + cw])
        slabs = [st_ref[k, :, c0:c0 + cw] for k in range(width - 1)]
        slabs += [g[t * nb:(t + 1) * nb] for t in range(n_new)]
        outs = _conv_slabs(slabs, cw_ref[:, c0:c0 + cw], cb_ref[:, c0:c0 + cw], n_new)
        for t in range(n_new):
            act_ref[t * nb:(t + 1) * nb, c0:c0 + cw] = \
                (_silu(outs[t]) * up[t * nb:(t + 1) * nb]).astype(act_ref.dtype)
        for k in range(width - 1):
            nst_ref[k, :, c0:c0 + cw] = slabs[n_new + k]


def _ffn_in_step(x_tm, st_tm, nw, wg, wu, conv_w, conv_b, n_new):
    m, d = x_tm.shape
    nf = wg.shape[1]
    args = (x_tm, st_tm, nw, wg, wu, conv_w, conv_b)
    return pl.pallas_call(
        functools.partial(_ffn_in_step_kernel, n_new),
        grid=(1,),
        in_specs=[_const_spec(a.shape) for a in args],
        out_specs=[_const_spec((m, nf)), _const_spec(st_tm.shape)],
        out_shape=[jax.ShapeDtypeStruct((m, nf), BF16), jax.ShapeDtypeStruct(st_tm.shape, F32)],
        compiler_params=_cparams("arbitrary"),
        name="ffn_in_step",
    )(*args)


def _kvf_kernel(x_ref, nw_ref, wk_ref, wv_ref, wf_ref, bf_ref, tri_ref,
                k_ref, v_ref, kb_ref, vb_ref, lf_ref, fc_ref, xn_scr, carry_scr):
    li = pl.program_id(1)

    @pl.when(li == 0)
    def _():
        carry_scr[...] = jnp.zeros_like(carry_scr)

    xn_scr[...] = _rms(x_ref[0], nw_ref[...]).astype(BF16)
    xn = xn_scr[...]
    for c0, cw in _col_chunks(wk_ref.shape[1]):
        k = _dot(xn, wk_ref[:, c0:c0 + cw])
        k_ref[0, :, c0:c0 + cw] = k
        kb_ref[0, :, c0:c0 + cw] = k.astype(BF16)
        v = _dot(xn, wv_ref[:, c0:c0 + cw])
        v_ref[0, :, c0:c0 + cw] = v
        vb_ref[0, :, c0:c0 + cw] = v.astype(BF16)
    f = _dot(xn, wf_ref[...]) + bf_ref[...]
    lf = -_softplus(-f)
    lf_ref[0] = lf
    fc = _dot3_sel_x(tri_ref[...], lf) + carry_scr[0:1, :]
    fc_ref[0] = fc
    tm = fc.shape[0]
    carry_scr[...] = jnp.broadcast_to(fc[tm - 1:tm, :], carry_scr.shape)


def _kvf(h, nw, wk, wv, wf, bf, tri, tm):
    bsz, seq, d = h.shape
    nk = wk.shape[1]
    assert seq % tm == 0
    row = lambda n: pl.BlockSpec((1, tm, n), lambda b, l: (b, l, 0))
    ins = (h, nw, wk, wv, wf, bf, tri)
    return pl.pallas_call(
        _kvf_kernel,
        grid=(bsz, seq // tm),
        in_specs=[row(d)] + [_const_spec(a.shape) for a in ins[1:]],
        out_specs=[row(nk), row(nk), row(nk), row(nk), row(LANES), row(LANES)],
        out_shape=[jax.ShapeDtypeStruct((bsz, seq, nk), F32), jax.ShapeDtypeStruct((bsz, seq, nk), F32),
                   jax.ShapeDtypeStruct((bsz, seq, nk), BF16), jax.ShapeDtypeStruct((bsz, seq, nk), BF16),
                   jax.ShapeDtypeStruct((bsz, seq, LANES), F32), jax.ShapeDtypeStruct((bsz, seq, LANES), F32)],
        scratch_shapes=[pltpu.VMEM((tm, d), BF16), pltpu.VMEM((SUBLANES, LANES), F32)],
        compiler_params=_cparams("arbitrary", "arbitrary"),
        name="kvf_proj",
    )(*ins)


def _q_proj_kernel(scale, x_ref, nw_ref, w_ref, q_ref):
    xn = _rms(x_ref[...], nw_ref[...]).astype(BF16)
    for c0, cw in _col_chunks(w_ref.shape[1]):
        q_ref[:, c0:c0 + cw] = (_dot(xn, w_ref[:, c0:c0 + cw]) * scale).astype(q_ref.dtype)


def _q_proj(h, nw, w, scale, tm, out_dtype):
    m, d = h.shape
    n = w.shape[1]
    assert m % tm == 0
    row = lambda c: pl.BlockSpec((tm, c), lambda i: (i, 0))
    return pl.pallas_call(
        functools.partial(_q_proj_kernel, scale),
        grid=(m // tm,),
        in_specs=[row(d), _const_spec(nw.shape), _const_spec(w.shape)],
        out_specs=row(n),
        out_shape=jax.ShapeDtypeStruct((m, n), out_dtype),
        compiler_params=_cparams("arbitrary"),
        name="q_proj",
    )(h, nw, w)


LOG2E = 1.4426950408889634
N_SPLIT = 3
KV_BUILD_ROWS = 512


def _bias_selectors(hp, key_side):
    hrow = lax.broadcasted_iota(jnp.int32, (LANES, LANES), 0)
    lcol = lax.broadcasted_iota(jnp.int32, (LANES, LANES), 1)
    off = N_SPLIT if key_side else 0
    sels = []
    for i in range(2):
        per_head = []
        for r in range(N_SPLIT):
            hit = jnp.where(hrow == 2 * hp + i, lcol, -1) == 2 * N_SPLIT * i + off + r
            per_head.append(jnp.where(hit, 1.0, 0.0).astype(BF16))
        sels.append(per_head)
    lane = lax.broadcasted_iota(jnp.int32, (1, LANES), 1)
    ones = []
    for i in range(2):
        lo = 2 * N_SPLIT * i + (0 if key_side else N_SPLIT)
        ones.append(jnp.where((lane >= lo) & (lane < lo + N_SPLIT), 1.0, 0.0))
    return sels, ones


def _fox_prompt_kernel(q_ref, k_ref, v_ref, fc_ref, o_ref, kx_scr, qx_scr, sa_scr, sb_scr, m0_scr, m1_scr, l_scr,
                       acc_scr):
    hp = pl.program_id(1)
    qi = pl.program_id(2)
    tq = q_ref.shape[1]
    tk = tq
    seq = k_ref.shape[1]
    lane = lax.broadcasted_iota(jnp.int32, (tq, LANES), 1)
    first = lane < ATT_HEAD_DIM

    @pl.when(qi == 0)
    def _():
        sels, ones = _bias_selectors(hp, True)
        both = [sels[0][r] + sels[1][r] for r in range(N_SPLIT)]
        one_row = ones[0] + ones[1]

        def build(c, carry):
            r0 = pl.multiple_of(c * KV_BUILD_ROWS, KV_BUILD_ROWS)
            parts = _split3(-(fc_ref[0, pl.ds(r0, KV_BUILD_ROWS), :] * LOG2E))
            ext = one_row + _dot(parts[0], both[0])
            for r in range(1, N_SPLIT):
                ext = ext + _dot(parts[r], both[r])
            kx_scr[pl.ds(r0, KV_BUILD_ROWS), 0:LANES] = k_ref[0, pl.ds(r0, KV_BUILD_ROWS), :]
            kx_scr[pl.ds(r0, KV_BUILD_ROWS), LANES:2 * LANES] = ext.astype(BF16)
            return carry

        lax.fori_loop(0, seq // KV_BUILD_ROWS, build, 0)

    q = q_ref[0]
    zq = jnp.zeros_like(q)
    sels, ones = _bias_selectors(hp, False)
    q0 = pl.multiple_of(qi * tq, tq)
    fparts = _split3(fc_ref[0, pl.ds(q0, tq), :] * LOG2E)
    for i in range(2):
        ext = ones[i] + _dot(fparts[0], sels[i][0])
        for r in range(1, N_SPLIT):
            ext = ext + _dot(fparts[r], sels[i][r])
        qh = jnp.where(first, q, zq) if i == 0 else jnp.where(first, zq, q)
        qx_scr[i] = jnp.concatenate([qh, ext.astype(BF16)], axis=1)

    m0_scr[...] = jnp.full_like(m0_scr, -jnp.inf)
    m1_scr[...] = jnp.full_like(m1_scr, -jnp.inf)
    l_scr[...] = jnp.zeros_like(l_scr)
    acc_scr[...] = jnp.zeros_like(acc_scr)

    first_k = lax.broadcasted_iota(jnp.int32, (tk, LANES), 1) < ATT_HEAD_DIM
    ones_top = jnp.where(first_k, 1.0, 0.0).astype(BF16)
    ones_bot = jnp.where(first_k, 0.0, 1.0).astype(BF16)
    m_scrs = (m0_scr, m1_scr)

    def qk_into(buf, kj):
        start = pl.multiple_of(kj * tk, tk)
        kx = kx_scr[pl.ds(start, tk), :]
        for i in range(2):
            buf[i * tq:(i + 1) * tq, :] = _dot_nt(qx_scr[i], kx)

    def consume(buf, kj, masked):
        start = pl.multiple_of(kj * tk, tk)
        v = v_ref[0, pl.ds(start, tk), :]
        zv = jnp.zeros_like(v)
        vx = jnp.concatenate([jnp.concatenate([jnp.where(first_k, v, zv), ones_top], axis=1),
                              jnp.concatenate([jnp.where(first_k, zv, v), ones_bot], axis=1)], axis=0)
        ps, alphas = [], []
        for i in range(2):
            chunks = [buf[i * tq:(i + 1) * tq, c0:c0 + LANES] for c0 in range(0, tk, LANES)]
            if masked:
                r = lax.broadcasted_iota(jnp.int32, (tq, LANES), 0)
                c = lax.broadcasted_iota(jnp.int32, (tq, LANES), 1)
                chunks = [jnp.where(r >= c + c0, ch, -jnp.inf) for ch, c0 in zip(chunks, range(0, tk, LANES))]
            m_old = m_scrs[i][...]
            m_new = jnp.maximum(m_old, jnp.max(functools.reduce(jnp.maximum, chunks), axis=1, keepdims=True))
            alphas.append(jnp.exp2(m_old - m_new))
            ps.append(jnp.concatenate([jnp.exp2(ch - m_new).astype(BF16) for ch in chunks], axis=1))
            m_scrs[i][...] = m_new
        tot = _dot(jnp.concatenate(ps, axis=1), vx)
        alpha = jnp.where(first, alphas[0], alphas[1])
        acc_scr[...] = alpha * acc_scr[...] + tot[:, :LANES]
        l_scr[...] = alpha * l_scr[...] + tot[:, LANES:]

    qk_into(sa_scr, 0)
    n_pairs = qi // 2

    def body(t, c):
        qk_into(sb_scr, 2 * t + 1)
        consume(sa_scr, 2 * t, False)
        qk_into(sa_scr, 2 * t + 2)
        consume(sb_scr, 2 * t + 1, False)
        return c

    lax.fori_loop(0, n_pairs, body, 0)

    @pl.when(qi % 2 == 0)
    def _():
        consume(sa_scr, qi, True)

    @pl.when(qi % 2 == 1)
    def _():
        qk_into(sb_scr, qi)
        consume(sa_scr, qi - 1, False)
        consume(sb_scr, qi, True)

    o_ref[0] = (acc_scr[...] / l_scr[...]).astype(o_ref.dtype)


def _fox_prompt(q, kb, vb, fc, tq):
    bsz, seq, dm = q.shape
    n_pairs = dm // LANES
    assert seq % tq == 0 and seq % KV_BUILD_ROWS == 0
    nq = seq // tq
    return pl.pallas_call(
        _fox_prompt_kernel,
        grid=(bsz, n_pairs, nq),
        in_specs=[pl.BlockSpec((1, tq, LANES), lambda b, p, i: (b, i, p)),
                  pl.BlockSpec((1, seq, LANES), lambda b, p, i: (b, 0, p)),
                  pl.BlockSpec((1, seq, LANES), lambda b, p, i: (b, 0, p)),
                  pl.BlockSpec((1, seq, LANES), lambda b, p, i: (b, 0, 0))],
        out_specs=pl.BlockSpec((1, tq, LANES), lambda b, p, i: (b, i, p)),
        out_shape=jax.ShapeDtypeStruct((bsz, seq, dm), BF16),
        scratch_shapes=[pltpu.VMEM((seq, 2 * LANES), BF16), pltpu.VMEM((2, tq, 2 * LANES), BF16),
                        pltpu.VMEM((2 * tq, tq), F32), pltpu.VMEM((2 * tq, tq), F32),
                        pltpu.VMEM((tq, LANES), F32), pltpu.VMEM((tq, LANES), F32),
                        pltpu.VMEM((tq, LANES), F32), pltpu.VMEM((tq, LANES), F32)],
        compiler_params=_cparams("arbitrary", "arbitrary", "arbitrary"),
        name="fox_prompt",
    )(q, kb, vb, fc)


def _fox_sample_kernel(n_group, n_new, pt_ref, q_ref, kn_ref, vn_ref, lfn_ref, suf_ref, *refs):
    k_refs = refs[0:n_group]
    v_refs = refs[n_group:2 * n_group]
    lf_refs = refs[2 * n_group:3 * n_group]
    o_ref, m_scr, l_scr, acc_scr, fn_scr, carry_scr = refs[3 * n_group:]
    i = pl.program_id(1)
    n_heads, n_rows, dh = q_ref.shape[1:]
    page = suf_ref.shape[0]

    @pl.when(i == 0)
    def _():
        y = lfn_ref[0]
        t3 = lax.broadcasted_iota(jnp.int32, y.shape, 1)
        sh = 1
        while sh < n_new:
            y = y + jnp.where(t3 >= sh, pltpu.roll(y, sh, 1), 0.0)
            sh *= 2
        fn_scr[...] = y
        q = q_ref[0]
        t1 = lax.broadcasted_iota(jnp.int32, (n_heads, n_rows, 1), 1)
        fcol = y[:, :, 0:1]
        ss = []
        for u in range(n_new):
            s_u = jnp.sum(q * kn_ref[0, :, u:u + 1, :], axis=2, keepdims=True) + fcol - fcol[:, u:u + 1, :]
            ss.append(jnp.where(t1 >= u, s_u, -jnp.inf))
        m0 = functools.reduce(jnp.maximum, ss)
        l0 = jnp.zeros_like(m0)
        acc0 = jnp.zeros((n_heads, n_rows, dh), F32)
        for u in range(n_new):
            p_u = jnp.exp(ss[u] - m0)
            l0 = l0 + p_u
            acc0 = acc0 + p_u * vn_ref[0, :, u:u + 1, :]
        m_scr[...] = jnp.broadcast_to(m0, m_scr.shape)
        l_scr[...] = jnp.broadcast_to(l0, l_scr.shape)
        acc_scr[...] = acc0
        carry_scr[...] = jnp.zeros_like(carry_scr)

    suf = suf_ref[...]
    withins, prefixes = [], []
    run = carry_scr[:, 0:1]
    for g in range(n_group):
        lf_t = lf_refs[g][0]
        withins.append(_dot3_x_sel(lf_t, suf))
        prefixes.append(run)
        run = run + jnp.sum(lf_t, axis=1, keepdims=True)
    carry_scr[...] = jnp.broadcast_to(run, carry_scr.shape)

    qs = [q_ref[0, h].astype(BF16) for h in range(n_heads)]
    fn = fn_scr[...]
    for g in range(n_group):
        shift = withins[g] + prefixes[g]
        s3 = jnp.stack([
            _dot_nt(qs[h], k_refs[g][0, pl.ds(h, page, stride=n_heads), :].astype(BF16)) + shift[h:h + 1, :]
            for h in range(n_heads)]) + fn
        m_old = m_scr[...]
        m_new = jnp.maximum(m_old, jnp.max(s3, axis=2, keepdims=True))
        alpha = jnp.exp(m_old - m_new)
        p3 = jnp.exp(s3 - m_new)
        l_scr[...] = alpha * l_scr[...] + jnp.sum(p3, axis=2, keepdims=True)
        m_scr[...] = m_new
        pb = p3.astype(BF16)
        pv = jnp.stack([
            _dot(pb[h], v_refs[g][0, pl.ds(h, page, stride=n_heads), :].astype(BF16))
            for h in range(n_heads)])
        acc_scr[...] = alpha[:, :, 0:dh] * acc_scr[...] + pv

    @pl.when(i == pl.num_programs(1) - 1)
    def _():
        o_ref[0] = acc_scr[...] / l_scr[:, :, 0:dh]


def _fox_sample(q4, kn4, vn4, lfn4, cache_k, cache_v, cache_lf_t, page_table, suf, n_group, n_new):
    db, n_heads, n_rows, dh = q4.shape
    n_pages = page_table.shape[1]
    page = cache_lf_t.shape[2]
    assert n_pages % n_group == 0 and page == LANES and n_new <= n_rows
    pt_flat = page_table.reshape(-1)

    def page_map(g):
        def imap(b, i, pt):
            return (pt[b * n_pages + (n_pages - 1 - (i * n_group + g))], 0, 0)
        return imap

    seq_spec = lambda c: pl.BlockSpec((1, n_heads, n_rows, c), lambda b, i, pt: (b, 0, 0, 0))
    in_specs = [seq_spec(dh), seq_spec(dh), seq_spec(dh), seq_spec(LANES),
                pl.BlockSpec(suf.shape, lambda b, i, pt: (0, 0))]
    in_specs += [pl.BlockSpec((1, page * n_heads, dh), page_map(g)) for g in range(n_group)]
    in_specs += [pl.BlockSpec((1, page * n_heads, dh), page_map(g)) for g in range(n_group)]
    in_specs += [pl.BlockSpec((1, n_heads, page), page_map(g)) for g in range(n_group)]
    grid_spec = pltpu.PrefetchScalarGridSpec(
        num_scalar_prefetch=1,
        grid=(db, n_pages // n_group),
        in_specs=in_specs,
        out_specs=seq_spec(dh),
        scratch_shapes=[pltpu.VMEM((n_heads, n_rows, LANES), F32), pltpu.VMEM((n_heads, n_rows, LANES), F32),
                        pltpu.VMEM((n_heads, n_rows, dh), F32), pltpu.VMEM((n_heads, n_rows, LANES), F32),
                        pltpu.VMEM((n_heads, LANES), F32)],
    )
    return pl.pallas_call(
        functools.partial(_fox_sample_kernel, n_group, n_new),
        grid_spec=grid_spec,
        out_shape=jax.ShapeDtypeStruct((db, n_heads, n_rows, dh), F32),
        compiler_params=_cparams("arbitrary", "arbitrary"),
        name="fox_sample",
    )(pt_flat, q4, kn4, vn4, lfn4, suf, *([cache_k] * n_group), *([cache_v] * n_group),
      *([cache_lf_t] * n_group))


def _prep_weights(norm_w, m_w_in, m_conv_w, m_conv_b, m_dt_bias, m_a_log, m_d, m_gnorm_w, m_w_out,
                  ffn_w_in, ffn_conv_w, ffn_conv_b, ffn_w_out, kv_norm_w, w_kvf, b_f, w_q, w_o):
    d_model = norm_w.shape[-1]
    d_inner = m_w_out.shape[1]
    n_sheads = m_dt_bias.shape[1]
    conv_dim = m_conv_w.shape[2]
    d_ff = ffn_w_out.shape[1]
    att_dim = w_q.shape[2]
    n_aheads = b_f.shape[0]
    pad_lanes = lambda a: jnp.pad(a, [(0, 0)] * (a.ndim - 1) + [(0, LANES - a.shape[-1])])
    w = {}
    w["norm"] = norm_w.reshape(norm_w.shape[0], norm_w.shape[1], 1, d_model)
    w["m_wz"] = m_w_in[:, :, :d_inner].astype(BF16)
    w["m_wx"] = m_w_in[:, :, d_inner:d_inner + conv_dim].astype(BF16)
    w["m_wdt"] = pad_lanes(m_w_in[:, :, d_inner + conv_dim:]).astype(BF16)
    w["m_conv_w"] = m_conv_w
    w["m_conv_b"] = m_conv_b[:, None, :]
    w["m_dtb"] = pad_lanes(m_dt_bias)[:, None, :]
    w["m_alog"] = pad_lanes(m_a_log)[:, None, :]
    w["m_dsk"] = jnp.repeat(m_d.astype(F32), SSM_HEAD_DIM, axis=1)[:, None, :]
    w["m_gw"] = m_gnorm_w[:, None, :]
    w["m_wout"] = m_w_out.astype(BF16)
    w["f_wg"] = ffn_w_in[:, :, :d_ff].astype(BF16)
    w["f_wu"] = ffn_w_in[:, :, d_ff:].astype(BF16)
    w["f_conv_w"] = ffn_conv_w
    w["f_conv_b"] = ffn_conv_b[:, None, :]
    w["f_wout"] = ffn_w_out.astype(BF16)
    w["kv_norm"] = kv_norm_w[None, :]
    w["wk"] = w_kvf[:, :att_dim].astype(BF16)
    w["wv"] = w_kvf[:, att_dim:2 * att_dim].astype(BF16)
    w["wf"] = pad_lanes(w_kvf[:, 2 * att_dim:]).astype(BF16)
    w["bf"] = pad_lanes(b_f[None, :])
    w["wq"] = w_q.astype(BF16)
    w["wo"] = w_o.astype(BF16)
    hh = jnp.arange(LANES)[:, None]
    cc = jnp.arange(d_inner)[None, :] // SSM_HEAD_DIM
    w["ex"] = (hh == cc).astype(BF16)
    r = jnp.arange(SSD_CHUNK)
    w["tri_chunk"] = (r[:, None] >= r[None, :]).astype(BF16)
    p = jnp.arange(PAGE_SIZE)
    w["suf"] = (p[:, None] > p[None, :]).astype(BF16)
    w["n_sheads"] = n_sheads
    w["n_aheads"] = n_aheads
    return w


TM_SEQ = 512
TM_MAMBA_IN = 256
TQ = 512
N_PAGE_GROUP = 4


def _tri(n):
    r = jnp.arange(n)
    return (r[:, None] >= r[None, :]).astype(BF16)


def _prompt_trunk(x, w, depth, n_a):
    bsz, seq, d = x.shape
    m = bsz * seq
    h = x
    d_inner = w["m_wout"].shape[1]
    outs_ssm, outs_mconv, outs_ffn = [], [], []
    kf = vf = kb = vb = lf = fc = None
    for i in range(depth):
        nw = w["norm"][i]
        if i < n_a:
            z, xa, dt, cst = _mamba_in_seq(h, nw[0], w["m_wz"][i], w["m_wx"][i], w["m_wdt"][i],
                                           w["m_conv_w"][i], w["m_conv_b"][i], w["m_dtb"][i], TM_MAMBA_IN)
            s0 = jnp.zeros((bsz, d_inner, D_STATE), F32)
            y, s_new = _ssd(xa, z, dt, s0, w["m_alog"][i], w["m_dsk"][i], w["ex"], w["tri_chunk"], SSD_CHUNK)
            h = _out_proj(y.reshape(m, d_inner), w["m_wout"][i], h.reshape(m, d), nw[1], TM_SEQ,
                          gw=w["m_gw"][i]).reshape(bsz, seq, d)
            width = w["m_conv_w"].shape[1]
            outs_mconv.append(cst[:, SUBLANES - (width - 1):, :])
            outs_ssm.append(s_new.reshape(bsz, w["n_sheads"], SSM_HEAD_DIM, D_STATE))
        else:
            j = i - n_a
            q = _q_proj(h.reshape(m, d), nw[0], w["wq"][j], ATT_HEAD_DIM ** -0.5 * LOG2E, TM_SEQ, BF16)
            o = _fox_prompt(q.reshape(bsz, seq, -1), kb, vb, fc, TQ)
            h = _out_proj(o.reshape(m, -1), w["wo"][j], h.reshape(m, d), nw[1], TM_SEQ).reshape(bsz, seq, d)
        act, fst = _ffn_in_seq(h, nw[2], w["f_wg"][i], w["f_wu"][i], w["f_conv_w"][i], w["f_conv_b"][i], TM_SEQ)
        h = _out_proj(act.reshape(m, -1), w["f_wout"][i], h.reshape(m, d), nw[3], TM_SEQ).reshape(bsz, seq, d)
        fwidth = w["f_conv_w"].shape[1]
        outs_ffn.append(fst[:, SUBLANES - (fwidth - 1):, :])
        if i == n_a - 1:
            kf, vf, kb, vb, lf, fc = _kvf(h, w["kv_norm"], w["wk"], w["wv"], w["wf"], w["bf"], _tri(TM_SEQ), TM_SEQ)
    n_ah = w["n_aheads"]
    return (h, jnp.stack(outs_ssm), jnp.stack(outs_mconv), jnp.stack(outs_ffn),
            kf.reshape(bsz, seq, n_ah, ATT_HEAD_DIM), vf.reshape(bsz, seq, n_ah, ATT_HEAD_DIM), lf[:, :, :n_ah])


def _to_tm(a):
    db, n_new, c = a.shape
    return jnp.swapaxes(a, 0, 1).reshape(n_new * db, c)


def _from_tm(a, db):
    m, c = a.shape
    return jnp.swapaxes(a.reshape(m // db, db, c), 0, 1)


def _heads_major(a, n_heads):
    db, n_new, c = a.shape
    a4 = jnp.swapaxes(a.reshape(db, n_new, n_heads, c // n_heads), 1, 2)
    return jnp.pad(a4, ((0, 0), (0, 0), (0, SUBLANES - n_new), (0, 0)))


def _sample_trunk(x, ssm_st, mconv_st, ffn_st, cache_k, cache_v, cache_logf, page_table, w, depth, n_a):
    db, n_new, d = x.shape
    m = db * n_new
    h = _to_tm(x)
    d_inner = w["m_wout"].shape[1]
    n_ah = w["n_aheads"]
    outs_ssm, outs_mconv, outs_ffn = [], [], []
    kf = vf = lf = None
    pad_t = lambda a: jnp.pad(a, ((0, 0), (0, SSD_CHUNK - n_new), (0, 0)))
    for i in range(depth):
        nw = w["norm"][i]
        if i < n_a:
            st_tm = jnp.swapaxes(mconv_st[i], 0, 1)
            z, xa, dt, nst = _mamba_in_step(h, st_tm, nw[0], w["m_wz"][i], w["m_wx"][i], w["m_wdt"][i],
                                            w["m_conv_w"][i], w["m_conv_b"][i], w["m_dtb"][i], n_new)
            s0 = ssm_st[i].reshape(db, d_inner, D_STATE)
            y, s_new = _ssd(pad_t(_from_tm(xa, db)), pad_t(_from_tm(z, db)), pad_t(_from_tm(dt, db)), s0,
                            w["m_alog"][i], w["m_dsk"][i], w["ex"], w["tri_chunk"], n_new)
            y_tm = _to_tm(y[:, :n_new, :])
            h = _out_proj(y_tm, w["m_wout"][i], h, nw[1], m, gw=w["m_gw"][i])
            outs_mconv.append(jnp.swapaxes(nst, 0, 1))
            outs_ssm.append(s_new.reshape(db, w["n_sheads"], SSM_HEAD_DIM, D_STATE))
        else:
            j = i - n_a
            q = _from_tm(_q_proj(h, nw[0], w["wq"][j], ATT_HEAD_DIM ** -0.5, m, F32), db)
            pool = cache_k.shape[0]
            ck = cache_k.reshape(pool, PAGE_SIZE * n_ah, ATT_HEAD_DIM)
            cv = cache_v.reshape(pool, PAGE_SIZE * n_ah, ATT_HEAD_DIM)
            clf_t = jnp.swapaxes(cache_logf, 1, 2)
            lfn3 = jnp.pad(jnp.swapaxes(lf[:, :, :n_ah], 1, 2), ((0, 0), (0, 0), (0, SUBLANES - n_new)))
            lfn4 = jnp.broadcast_to(lfn3[..., None], (db, n_ah, SUBLANES, LANES))
            o4 = _fox_sample(_heads_major(q, n_ah), _heads_major(kf, n_ah), _heads_major(vf, n_ah), lfn4,
                             ck, cv, clf_t, page_table, w["suf"], N_PAGE_GROUP, n_new)
            o = jnp.swapaxes(o4[:, :, :n_new, :], 1, 2).reshape(db, n_new, -1)
            h = _out_proj(_to_tm(o), w["wo"][j], h, nw[1], m)
        fst_tm = jnp.swapaxes(ffn_st[i], 0, 1)
        act, nfst = _ffn_in_step(h, fst_tm, nw[2], w["f_wg"][i], w["f_wu"][i], w["f_conv_w"][i],
                                 w["f_conv_b"][i], n_new)
        h = _out_proj(act, w["f_wout"][i], h, nw[3], m)
        outs_ffn.append(jnp.swapaxes(nfst, 0, 1))
        if i == n_a - 1:
            kf_tm, vf_tm, _, _, lf_tm, _ = _kvf(h[None], w["kv_norm"], w["wk"], w["wv"], w["wf"], w["bf"],
                                                _tri(m), m)
            kf, vf, lf = _from_tm(kf_tm[0], db), _from_tm(vf_tm[0], db), _from_tm(lf_tm[0], db)
    return (_from_tm(h, db), jnp.stack(outs_ssm), jnp.stack(outs_mconv), jnp.stack(outs_ffn),
            kf.reshape(db, n_new, n_ah, ATT_HEAD_DIM), vf.reshape(db, n_new, n_ah, ATT_HEAD_DIM), lf[:, :, :n_ah])


def kernel(x_prompt, x_sample, state_ssm, state_mconv, state_ffn, cache_k, cache_v, cache_logf, page_table,
           norm_w, m_w_in, m_conv_w, m_conv_b, m_dt_bias, m_a_log, m_d, m_gnorm_w, m_w_out,
           ffn_w_in, ffn_conv_w, ffn_conv_b, ffn_w_out, kv_norm_w, w_kvf, b_f, w_q, w_o):
    depth = norm_w.shape[0]
    n_a = m_w_in.shape[0]
    w = _prep_weights(norm_w, m_w_in, m_conv_w, m_conv_b, m_dt_bias, m_a_log, m_d, m_gnorm_w, m_w_out,
                      ffn_w_in, ffn_conv_w, ffn_conv_b, ffn_w_out, kv_norm_w, w_kvf, b_f, w_q, w_o)
    y_p, p_ssm, p_mconv, p_ffn, p_k, p_v, p_lf = _prompt_trunk(x_prompt, w, depth, n_a)
    y_s, s_ssm, s_mconv, s_ffn, s_k, s_v, s_lf = _sample_trunk(
        x_sample, state_ssm, state_mconv, state_ffn, cache_k, cache_v, cache_logf, page_table, w, depth, n_a)
    return (y_p, y_s, p_ssm, p_mconv, p_ffn, p_k, p_v, p_lf, s_ssm, s_mconv, s_ffn, s_k, s_v, s_lf)
```

```python
import functools
import math

import jax
import jax.numpy as jnp
from jax import lax
from jax.experimental import pallas as pl
from jax.experimental.pallas import tpu as pltpu

F32 = jnp.float32
BF16 = jnp.bfloat16
EPS = 1e-6

LANES = 128
SUBLANES = 8
VMEM_LIMIT_BYTES = 56 * 1024 * 1024

SSM_HEAD_DIM = 64
SSM_GROUPS = 4
D_STATE = 128
SSD_CHUNK = 128
ATT_HEAD_DIM = 64
PAGE_SIZE = 128


def _cparams(*sem):
    return pltpu.CompilerParams(dimension_semantics=sem, vmem_limit_bytes=VMEM_LIMIT_BYTES)


def _const_spec(shape):
    nd = len(shape)
    return pl.BlockSpec(shape, lambda *_: (0,) * nd)


def _rms(x, w):
    ms = jnp.mean(x * x, axis=-1, keepdims=True)
    return x * lax.rsqrt(ms + EPS) * w


def _silu(x):
    return x / (1.0 + jnp.exp(-x))


def _softplus(x):
    return jnp.maximum(x, 0.0) + jnp.log1p(jnp.exp(-jnp.abs(x)))


def _dot(a, b):
    return jnp.dot(a, b, preferred_element_type=F32)


def _dot_nt(a, b):
    return lax.dot_general(a, b, (((1,), (1,)), ((), ())), preferred_element_type=F32)


def _dot_tn(a, b):
    return lax.dot_general(a, b, (((0,), (0,)), ((), ())), preferred_element_type=F32)


def _split3(x):
    hi = x.astype(BF16)
    r1 = x - hi.astype(F32)
    mid = r1.astype(BF16)
    lo = (r1 - mid.astype(F32)).astype(BF16)
    return hi, mid, lo


def _dot3_x_sel(x, sel):
    hi, mid, lo = _split3(x)
    return _dot(hi, sel) + _dot(mid, sel) + _dot(lo, sel)


def _dot3_sel_x(sel, x):
    hi, mid, lo = _split3(x)
    return _dot(sel, hi) + _dot(sel, mid) + _dot(sel, lo)


def _delayed_rows(g, carry8, j):
    if j == 0:
        return g
    r = pltpu.roll(g, j, 0)
    c = pltpu.roll(carry8, j, 0)
    rows = lax.broadcasted_iota(jnp.int32, c.shape, 0)
    head = jnp.where(rows < j, c, r[:SUBLANES])
    return jnp.concatenate([head, r[SUBLANES:]], axis=0)


def _conv_rows(g, carry8, w, b):
    width = w.shape[0]
    out = b + _delayed_rows(g, carry8, width - 1) * w[0:1]
    for k in range(1, width):
        out = out + _delayed_rows(g, carry8, width - 1 - k) * w[k:k + 1]
    return out


def _conv_slabs(slabs, w, b, n_new):
    width = w.shape[0]
    outs = []
    for t in range(n_new):
        o = b + slabs[t] * w[0:1]
        for k in range(1, width):
            o = o + slabs[t + k] * w[k:k + 1]
        outs.append(o)
    return outs


def _col_chunks(n, step=512):
    out, c = [], 0
    while c < n:
        w = min(step, n - c)
        out.append((c, w))
        c += w
    return out


def _mamba_in_seq_kernel(x_ref, nw_ref, wz_ref, wx_ref, wdt_ref, cw_ref, cb_ref, dtb_ref,
                         z_ref, xa_ref, dt_ref, st_ref, xn_scr, carry_scr):
    li = pl.program_id(1)

    @pl.when(li == 0)
    def _():
        carry_scr[...] = jnp.zeros_like(carry_scr)

    xn_scr[...] = _rms(x_ref[0], nw_ref[...]).astype(BF16)
    xn = xn_scr[...]
    tm = xn.shape[0]
    for c0, cw in _col_chunks(wz_ref.shape[1]):
        z_ref[0, :, c0:c0 + cw] = _dot(xn, wz_ref[:, c0:c0 + cw])
    for c0, cw in _col_chunks(wx_ref.shape[1]):
        g = _dot(xn, wx_ref[:, c0:c0 + cw])
        carry8 = carry_scr[:, c0:c0 + cw]
        y = _conv_rows(g, carry8, cw_ref[:, c0:c0 + cw], cb_ref[:, c0:c0 + cw])
        xa_ref[0, :, c0:c0 + cw] = _silu(y)
        carry_scr[:, c0:c0 + cw] = g[tm - SUBLANES:]
    dt_ref[0] = _softplus(_dot(xn, wdt_ref[...]) + dtb_ref[...])
    st_ref[0] = carry_scr[...]


def _mamba_in_seq(h, nw, wz, wx, wdt, conv_w, conv_b, dtb, tm):
    bsz, seq, d = h.shape
    nz, nx = wz.shape[1], wx.shape[1]
    assert seq % tm == 0
    grid = (bsz, seq // tm)
    row = lambda n: pl.BlockSpec((1, tm, n), lambda b, l: (b, l, 0))
    return pl.pallas_call(
        _mamba_in_seq_kernel,
        grid=grid,
        in_specs=[row(d), _const_spec(nw.shape), _const_spec(wz.shape), _const_spec(wx.shape),
                  _const_spec(wdt.shape), _const_spec(conv_w.shape), _const_spec(conv_b.shape),
                  _const_spec(dtb.shape)],
        out_specs=[row(nz), row(nx), row(LANES),
                   pl.BlockSpec((1, SUBLANES, nx), lambda b, l: (b, 0, 0))],
        out_shape=[jax.ShapeDtypeStruct((bsz, seq, nz), F32),
                   jax.ShapeDtypeStruct((bsz, seq, nx), F32),
                   jax.ShapeDtypeStruct((bsz, seq, LANES), F32),
                   jax.ShapeDtypeStruct((bsz, SUBLANES, nx), F32)],
        scratch_shapes=[pltpu.VMEM((tm, d), BF16), pltpu.VMEM((SUBLANES, nx), F32)],
        compiler_params=_cparams("arbitrary", "arbitrary"),
        name="mamba_in_seq",
    )(h, nw, wz, wx, wdt, conv_w, conv_b, dtb)


def _mamba_in_step_kernel(n_new, x_ref, st_ref, nw_ref, wz_ref, wx_ref, wdt_ref, cw_ref, cb_ref, dtb_ref,
                          z_ref, xa_ref, dt_ref, nst_ref):
    xn = _rms(x_ref[...], nw_ref[...]).astype(BF16)
    nb = xn.shape[0] // n_new
    width = cw_ref.shape[0]
    for c0, cw in _col_chunks(wz_ref.shape[1]):
        z_ref[:, c0:c0 + cw] = _dot(xn, wz_ref[:, c0:c0 + cw])
    for c0, cw in _col_chunks(wx_ref.shape[1]):
        g = _dot(xn, wx_ref[:, c0:c0 + cw])
        slabs = [st_ref[k, :, c0:c0 + cw] for k in range(width - 1)]
        slabs += [g[t * nb:(t + 1) * nb] for t in range(n_new)]
        outs = _conv_slabs(slabs, cw_ref[:, c0:c0 + cw], cb_ref[:, c0:c0 + cw], n_new)
        for t in range(n_new):
            xa_ref[t * nb:(t + 1) * nb, c0:c0 + cw] = _silu(outs[t])
        for k in range(width - 1):
            nst_ref[k, :, c0:c0 + cw] = slabs[n_new + k]
    dt_ref[...] = _softplus(_dot(xn, wdt_ref[...]) + dtb_ref[...])


def _mamba_in_step(x_tm, st_tm, nw, wz, wx, wdt, conv_w, conv_b, dtb, n_new):
    m, d = x_tm.shape
    nz, nx = wz.shape[1], wx.shape[1]
    args = (x_tm, st_tm, nw, wz, wx, wdt, conv_w, conv_b, dtb)
    return pl.pallas_call(
        functools.partial(_mamba_in_step_kernel, n_new),
        grid=(1,),
        in_specs=[_const_spec(a.shape) for a in args],
        out_specs=[_const_spec((m, nz)), _const_spec((m, nx)), _const_spec((m, LANES)),
                   _const_spec(st_tm.shape)],
        out_shape=[jax.ShapeDtypeStruct((m, nz), F32), jax.ShapeDtypeStruct((m, nx), F32),
                   jax.ShapeDtypeStruct((m, LANES), F32), jax.ShapeDtypeStruct(st_tm.shape, F32)],
        compiler_params=_cparams("arbitrary"),
        name="mamba_in_step",
    )(*args)


def _ssd_kernel(valid, xa_ref, z_ref, dt_ref, s0_ref, alog_ref, dsk_ref, ex_ref, tri_ref,
                y_ref, sout_ref, st_scr):
    ci = pl.program_id(1)
    t_rows = xa_ref.shape[1]
    d_inner = z_ref.shape[2]
    gn = SSM_GROUPS * D_STATE
    n_heads = d_inner // SSM_HEAD_DIM
    heads_per_group = n_heads // SSM_GROUPS
    gcols = heads_per_group * SSM_HEAD_DIM

    @pl.when(ci == 0)
    def _():
        for c0 in range(0, d_inner, LANES):
            st_scr[:, c0:c0 + LANES] = s0_ref[0, c0:c0 + LANES, :].T

    xs = xa_ref[0, :, 0:d_inner]
    bm = xa_ref[0, :, d_inner:d_inner + gn].astype(BF16)
    cm = xa_ref[0, :, d_inner + gn:d_inner + 2 * gn].astype(BF16)
    dt = dt_ref[0]
    rows = lax.broadcasted_iota(jnp.int32, dt.shape, 0)
    if valid < t_rows:
        dt = jnp.where(rows < valid, dt, 0.0)
    dta = dt * -jnp.exp(alog_ref[...])
    acum = _dot3_sel_x(tri_ref[...], dta)
    acum_t = acum.T
    ex = ex_ref[...]
    dt_e = _dot3_x_sel(dt, ex)
    acum_e = _dot3_x_sel(acum, ex)
    xdt = xs * dt_e
    xdt_b = xdt.astype(BF16)
    last_e = acum_e[t_rows - 1:t_rows, :]
    e_acum = jnp.exp(acum_e)
    xde_b = (xdt * jnp.exp(last_e - acum_e)).astype(BF16)
    e_last = jnp.exp(last_e)

    li = lax.broadcasted_iota(jnp.int32, (t_rows, t_rows), 0)
    si = lax.broadcasted_iota(jnp.int32, (t_rows, t_rows), 1)
    causal = li >= si
    lane = lax.broadcasted_iota(jnp.int32, (t_rows, LANES), 1)
    first_head = lane < SSM_HEAD_DIM

    for g in range(SSM_GROUPS):
        c_g = cm[:, g * D_STATE:(g + 1) * D_STATE]
        b_g = bm[:, g * D_STATE:(g + 1) * D_STATE]
        cb = _dot_nt(c_g, b_g)
        g0 = g * gcols
        pieces = []
        for k in range(heads_per_group // 2):
            h1 = g * heads_per_group + 2 * k
            ms = []
            for hh in (h1, h1 + 1):
                seg = acum[:, hh:hh + 1] - acum_t[hh:hh + 1, :]
                dec = jnp.exp(jnp.where(causal, seg, -jnp.inf))
                ms.append((cb * dec).astype(BF16))
            x2 = xdt_b[:, g0 + k * LANES:g0 + (k + 1) * LANES]
            zero = jnp.zeros_like(x2)
            rhs = jnp.concatenate([jnp.where(first_head, x2, zero), jnp.where(first_head, zero, x2)], axis=0)
            pieces.append(_dot(jnp.concatenate(ms, axis=1), rhs))
        y_diag = jnp.concatenate(pieces, axis=1)
        st_g = st_scr[:, g0:g0 + gcols]
        y_off = _dot(c_g, st_g.astype(BF16)) * e_acum[:, g0:g0 + gcols]
        xs_g = xs[:, g0:g0 + gcols]
        y = y_diag + y_off + dsk_ref[:, g0:g0 + gcols] * xs_g
        y_ref[0, :, g0:g0 + gcols] = (y * _silu(z_ref[0, :, g0:g0 + gcols])).astype(y_ref.dtype)
        st_scr[:, g0:g0 + gcols] = e_last[:, g0:g0 + gcols] * st_g + _dot_tn(b_g, xde_b[:, g0:g0 + gcols])

    @pl.when(ci == pl.num_programs(1) - 1)
    def _():
        for c0 in range(0, d_inner, LANES):
            sout_ref[0, c0:c0 + LANES, :] = st_scr[:, c0:c0 + LANES].T


def _ssd(xa, z, dt, s0, a_row, dsk_e, ex, tri, valid):
    bsz, seq, nx = xa.shape
    d_inner = z.shape[2]
    t_rows = SSD_CHUNK
    assert seq % t_rows == 0
    row = lambda n: pl.BlockSpec((1, t_rows, n), lambda b, c: (b, c, 0))
    st_spec = pl.BlockSpec((1, d_inner, D_STATE), lambda b, c: (b, 0, 0))
    consts = (a_row, dsk_e, ex, tri)
    return pl.pallas_call(
        functools.partial(_ssd_kernel, valid),
        grid=(bsz, seq // t_rows),
        in_specs=[row(nx), row(d_inner), row(LANES), st_spec] + [_const_spec(c.shape) for c in consts],
        out_specs=[row(d_inner), st_spec],
        out_shape=[jax.ShapeDtypeStruct((bsz, seq, d_inner), F32),
                   jax.ShapeDtypeStruct((bsz, d_inner, D_STATE), F32)],
        scratch_shapes=[pltpu.VMEM((D_STATE, d_inner), F32)],
        compiler_params=_cparams("arbitrary", "arbitrary"),
        name="ssd_chunk",
    )(xa, z, dt, s0, *consts)


def _out_proj_kernel(pre_norm, a_ref, *refs):
    if pre_norm:
        gw_ref, w_ref, h_ref, pw_ref, o_ref = refs
        a = _rms(a_ref[...].astype(F32), gw_ref[...]).astype(BF16)
    else:
        w_ref, h_ref, pw_ref, o_ref = refs
        a = a_ref[...].astype(BF16)
    o_ref[...] = h_ref[...] + _rms(_dot(a, w_ref[...]), pw_ref[...])


def _out_proj(a, w, h, pw, tm, gw=None):
    m, k = a.shape
    d = w.shape[1]
    assert m % tm == 0
    row = lambda n: pl.BlockSpec((tm, n), lambda i: (i, 0))
    ins = [a] + ([gw] if gw is not None else []) + [w, h, pw]
    specs = [row(k)] + ([_const_spec(gw.shape)] if gw is not None else []) + \
        [_const_spec(w.shape), row(d), _const_spec(pw.shape)]
    return pl.pallas_call(
        functools.partial(_out_proj_kernel, gw is not None),
        grid=(m // tm,),
        in_specs=specs,
        out_specs=row(d),
        out_shape=jax.ShapeDtypeStruct((m, d), F32),
        compiler_params=_cparams("arbitrary"),
        name="out_proj",
    )(*ins)


def _ffn_in_seq_kernel(x_ref, nw_ref, wg_ref, wu_ref, cw_ref, cb_ref, act_ref, st_ref, xn_scr, carry_scr):
    li = pl.program_id(1)

    @pl.when(li == 0)
    def _():
        carry_scr[...] = jnp.zeros_like(carry_scr)

    xn_scr[...] = _rms(x_ref[0], nw_ref[...]).astype(BF16)
    xn = xn_scr[...]
    tm = xn.shape[0]
    for c0, cw in _col_chunks(wg_ref.shape[1]):
        g = _dot(xn, wg_ref[:, c0:c0 + cw])
        up = _dot(xn, wu_ref[:, c0:c0 + cw])
        y = _conv_rows(g, carry_scr[:, c0:c0 + cw], cw_ref[:, c0:c0 + cw], cb_ref[:, c0:c0 + cw])
        act_ref[0, :, c0:c0 + cw] = (_silu(y) * up).astype(act_ref.dtype)
        carry_scr[:, c0:c0 + cw] = g[tm - SUBLANES:]
    st_ref[0] = carry_scr[...]


def _ffn_in_seq(h, nw, wg, wu, conv_w, conv_b, tm):
    bsz, seq, d = h.shape
    nf = wg.shape[1]
    assert seq % tm == 0
    row = lambda n: pl.BlockSpec((1, tm, n), lambda b, l: (b, l, 0))
    return pl.pallas_call(
        _ffn_in_seq_kernel,
        grid=(bsz, seq // tm),
        in_specs=[row(d), _const_spec(nw.shape), _const_spec(wg.shape), _const_spec(wu.shape),
                  _const_spec(conv_w.shape), _const_spec(conv_b.shape)],
        out_specs=[row(nf), pl.BlockSpec((1, SUBLANES, nf), lambda b, l: (b, 0, 0))],
        out_shape=[jax.ShapeDtypeStruct((bsz, seq, nf), BF16),
                   jax.ShapeDtypeStruct((bsz, SUBLANES, nf), F32)],
        scratch_shapes=[pltpu.VMEM((tm, d), BF16), pltpu.VMEM((SUBLANES, nf), F32)],
        compiler_params=_cparams("arbitrary", "arbitrary"),
        name="ffn_in_seq",
    )(h, nw, wg, wu, conv_w, conv_b)


def _ffn_in_step_kernel(n_new, x_ref, st_ref, nw_ref, wg_ref, wu_ref, cw_ref, cb_ref, act_ref, nst_ref):
    xn = _rms(x_ref[...], nw_ref[...]).astype(BF16)
    nb = xn.shape[0] // n_new
    width = cw_ref.shape[0]
    for c0, cw in _col_chunks(wg_ref.shape[1]):
        g = _dot(xn, wg_ref[:, c0:c0 + cw])
        up = _dot(xn, wu_ref[:, c0:c0 + cw])
        slabs = [st_ref[k, :, c0:c0 + cw] for k in range(width - 1)]
        slabs += [g[t * nb:(t + 1) * nb] for t in range(n_new)]
        outs = _conv_slabs(slabs, cw_ref[:, c0:c0 + cw], cb_ref[:, c0:c0 + cw], n_new)
        for t in range(n_new):
            act_ref[t * nb:(t + 1) * nb, c0:c0 + cw] = \
                (_silu(outs[t]) * up[t * nb:(t + 1) * nb]).astype(act_ref.dtype)
        for k in range(width - 1):
            nst_ref[k, :, c0:c0 + cw] = slabs[n_new + k]


def _ffn_in_step(x_tm, st_tm, nw, wg, wu, conv_w, conv_b, n_new):
    m, d = x_tm.shape
    nf = wg.shape[1]
    args = (x_tm, st_tm, nw, wg, wu, conv_w, conv_b)
    return pl.pallas_call(
        functools.partial(_ffn_in_step_kernel, n_new),
        grid=(1,),
        in_specs=[_const_spec(a.shape) for a in args],
        out_specs=[_const_spec((m, nf)), _const_spec(st_tm.shape)],
        out_shape=[jax.ShapeDtypeStruct((m, nf), BF16), jax.ShapeDtypeStruct(st_tm.shape, F32)],
        compiler_params=_cparams("arbitrary"),
        name="ffn_in_step",
    )(*args)


def _kvf_kernel(x_ref, nw_ref, wk_ref, wv_ref, wf_ref, bf_ref, tri_ref,
                k_ref, v_ref, kb_ref, vb_ref, lf_ref, fc_ref, xn_scr, carry_scr):
    li = pl.program_id(1)

    @pl.when(li == 0)
    def _():
        carry_scr[...] = jnp.zeros_like(carry_scr)

    xn_scr[...] = _rms(x_ref[0], nw_ref[...]).astype(BF16)
    xn = xn_scr[...]
    for c0, cw in _col_chunks(wk_ref.shape[1]):
        k = _dot(xn, wk_ref[:, c0:c0 + cw])
        k_ref[0, :, c0:c0 + cw] = k
        kb_ref[0, :, c0:c0 + cw] = k.astype(BF16)
        v = _dot(xn, wv_ref[:, c0:c0 + cw])
        v_ref[0, :, c0:c0 + cw] = v
        vb_ref[0, :, c0:c0 + cw] = v.astype(BF16)
    f = _dot(xn, wf_ref[...]) + bf_ref[...]
    lf = -_softplus(-f)
    lf_ref[0] = lf
    fc = _dot3_sel_x(tri_ref[...], lf) + carry_scr[0:1, :]
    fc_ref[0] = fc
    tm = fc.shape[0]
    carry_scr[...] = jnp.broadcast_to(fc[tm - 1:tm, :], carry_scr.shape)


def _kvf(h, nw, wk, wv, wf, bf, tri, tm):
    bsz, seq, d = h.shape
    nk = wk.shape[1]
    assert seq % tm == 0
    row = lambda n: pl.BlockSpec((1, tm, n), lambda b, l: (b, l, 0))
    ins = (h, nw, wk, wv, wf, bf, tri)
    return pl.pallas_call(
        _kvf_kernel,
        grid=(bsz, seq // tm),
        in_specs=[row(d)] + [_const_spec(a.shape) for a in ins[1:]],
        out_specs=[row(nk), row(nk), row(nk), row(nk), row(LANES), row(LANES)],
        out_shape=[jax.ShapeDtypeStruct((bsz, seq, nk), F32), jax.ShapeDtypeStruct((bsz, seq, nk), F32),
                   jax.ShapeDtypeStruct((bsz, seq, nk), BF16), jax.ShapeDtypeStruct((bsz, seq, nk), BF16),
                   jax.ShapeDtypeStruct((bsz, seq, LANES), F32), jax.ShapeDtypeStruct((bsz, seq, LANES), F32)],
        scratch_shapes=[pltpu.VMEM((tm, d), BF16), pltpu.VMEM((SUBLANES, LANES), F32)],
        compiler_params=_cparams("arbitrary", "arbitrary"),
        name="kvf_proj",
    )(*ins)


def _q_proj_kernel(scale, x_ref, nw_ref, w_ref, q_ref):
    xn = _rms(x_ref[...], nw_ref[...]).astype(BF16)
    for c0, cw in _col_chunks(w_ref.shape[1]):
        q_ref[:, c0:c0 + cw] = (_dot(xn, w_ref[:, c0:c0 + cw]) * scale).astype(q_ref.dtype)


def _q_proj(h, nw, w, scale, tm, out_dtype):
    m, d = h.shape
    n = w.shape[1]
    assert m % tm == 0
    row = lambda c: pl.BlockSpec((tm, c), lambda i: (i, 0))
    return pl.pallas_call(
        functools.partial(_q_proj_kernel, scale),
        grid=(m // tm,),
        in_specs=[row(d), _const_spec(nw.shape), _const_spec(w.shape)],
        out_specs=row(n),
        out_shape=jax.ShapeDtypeStruct((m, n), out_dtype),
        compiler_params=_cparams("arbitrary"),
        name="q_proj",
    )(h, nw, w)


LOG2E = 1.4426950408889634
N_SPLIT = 3
KV_BUILD_ROWS = 512


def _bias_selectors(hp, key_side):
    hrow = lax.broadcasted_iota(jnp.int32, (LANES, LANES), 0)
    lcol = lax.broadcasted_iota(jnp.int32, (LANES, LANES), 1)
    off = N_SPLIT if key_side else 0
    sels = []
    for i in range(2):
        per_head = []
        for r in range(N_SPLIT):
            hit = jnp.where(hrow == 2 * hp + i, lcol, -1) == 2 * N_SPLIT * i + off + r
            per_head.append(jnp.where(hit, 1.0, 0.0).astype(BF16))
        sels.append(per_head)
    lane = lax.broadcasted_iota(jnp.int32, (1, LANES), 1)
    ones = []
    for i in range(2):
        lo = 2 * N_SPLIT * i + (0 if key_side else N_SPLIT)
        ones.append(jnp.where((lane >= lo) & (lane < lo + N_SPLIT), 1.0, 0.0))
    return sels, ones


def _fox_prompt_kernel(q_ref, k_ref, v_ref, fc_ref, o_ref, kx_scr, qx_scr, sa_scr, sb_scr, m0_scr, m1_scr, l_scr,
                       acc_scr):
    hp = pl.program_id(1)
    qi = pl.program_id(2)
    tq = q_ref.shape[1]
    tk = tq
    seq = k_ref.shape[1]
    lane = lax.broadcasted_iota(jnp.int32, (tq, LANES), 1)
    first = lane < ATT_HEAD_DIM

    @pl.when(qi == 0)
    def _():
        sels, ones = _bias_selectors(hp, True)
        both = [sels[0][r] + sels[1][r] for r in range(N_SPLIT)]
        one_row = ones[0] + ones[1]

        def build(c, carry):
            r0 = pl.multiple_of(c * KV_BUILD_ROWS, KV_BUILD_ROWS)
            parts = _split3(-(fc_ref[0, pl.ds(r0, KV_BUILD_ROWS), :] * LOG2E))
            ext = one_row + _dot(parts[0], both[0])
            for r in range(1, N_SPLIT):
                ext = ext + _dot(parts[r], both[r])
            kx_scr[pl.ds(r0, KV_BUILD_ROWS), 0:LANES] = k_ref[0, pl.ds(r0, KV_BUILD_ROWS), :]
            kx_scr[pl.ds(r0, KV_BUILD_ROWS), LANES:2 * LANES] = ext.astype(BF16)
            return carry

        lax.fori_loop(0, seq // KV_BUILD_ROWS, build, 0)

    q = q_ref[0]
    zq = jnp.zeros_like(q)
    sels, ones = _bias_selectors(hp, False)
    q0 = pl.multiple_of(qi * tq, tq)
    fparts = _split3(fc_ref[0, pl.ds(q0, tq), :] * LOG2E)
    for i in range(2):
        ext = ones[i] + _dot(fparts[0], sels[i][0])
        for r in range(1, N_SPLIT):
            ext = ext + _dot(fparts[r], sels[i][r])
        qh = jnp.where(first, q, zq) if i == 0 else jnp.where(first, zq, q)
        qx_scr[i] = jnp.concatenate([qh, ext.astype(BF16)], axis=1)

    m0_scr[...] = jnp.full_like(m0_scr, -jnp.inf)
    m1_scr[...] = jnp.full_like(m1_scr, -jnp.inf)
    l_scr[...] = jnp.zeros_like(l_scr)
    acc_scr[...] = jnp.zeros_like(acc_scr)

    first_k = lax.broadcasted_iota(jnp.int32, (tk, LANES), 1) < ATT_HEAD_DIM
    ones_top = jnp.where(first_k, 1.0, 0.0).astype(BF16)
    ones_bot = jnp.where(first_k, 0.0, 1.0).astype(BF16)
    m_scrs = (m0_scr, m1_scr)

    def qk_into(buf, kj):
        start = pl.multiple_of(kj * tk, tk)
        kx = kx_scr[pl.ds(start, tk), :]
        for i in range(2):
            buf[i * tq:(i + 1) * tq, :] = _dot_nt(qx_scr[i], kx)

    def consume(buf, kj, masked):
        start = pl.multiple_of(kj * tk, tk)
        v = v_ref[0, pl.ds(start, tk), :]
        zv = jnp.zeros_like(v)
        vx = jnp.concatenate([jnp.concatenate([jnp.where(first_k, v, zv), ones_top], axis=1),
                              jnp.concatenate([jnp.where(first_k, zv, v), ones_bot], axis=1)], axis=0)
        ps, alphas = [], []
        for i in range(2):
            chunks = [buf[i * tq:(i + 1) * tq, c0:c0 + LANES] for c0 in range(0, tk, LANES)]
            if masked:
                r = lax.broadcasted_iota(jnp.int32, (tq, LANES), 0)
                c = lax.broadcasted_iota(jnp.int32, (tq, LANES), 1)
                chunks = [jnp.where(r >= c + c0, ch, -jnp.inf) for ch, c0 in zip(chunks, range(0, tk, LANES))]
            m_old = m_scrs[i][...]
            m_new = jnp.maximum(m_old, jnp.max(functools.reduce(jnp.maximum, chunks), axis=1, keepdims=True))
            alphas.append(jnp.exp2(m_old - m_new))
            ps.append(jnp.concatenate([jnp.exp2(ch - m_new).astype(BF16) for ch in chunks], axis=1))
            m_scrs[i][...] = m_new
        tot = _dot(jnp.concatenate(ps, axis=1), vx)
        alpha = jnp.where(first, alphas[0], alphas[1])
        acc_scr[...] = alpha * acc_scr[...] + tot[:, :LANES]
        l_scr[...] = alpha * l_scr[...] + tot[:, LANES:]

    qk_into(sa_scr, 0)
    n_pairs = qi // 2

    def body(t, c):
        qk_into(sb_scr, 2 * t + 1)
        consume(sa_scr, 2 * t, False)
        qk_into(sa_scr, 2 * t + 2)
        consume(sb_scr, 2 * t + 1, False)
        return c

    lax.fori_loop(0, n_pairs, body, 0)

    @pl.when(qi % 2 == 0)
    def _():
        consume(sa_scr, qi, True)

    @pl.when(qi % 2 == 1)
    def _():
        qk_into(sb_scr, qi)
        consume(sa_scr, qi - 1, False)
        consume(sb_scr, qi, True)

    o_ref[0] = (acc_scr[...] / l_scr[...]).astype(o_ref.dtype)


def _fox_prompt(q, kb, vb, fc, tq):
    bsz, seq, dm = q.shape
    n_pairs = dm // LANES
    assert seq % tq == 0 and seq % KV_BUILD_ROWS == 0
    nq = seq // tq
    return pl.pallas_call(
        _fox_prompt_kernel,
        grid=(bsz, n_pairs, nq),
        in_specs=[pl.BlockSpec((1, tq, LANES), lambda b, p, i: (b, i, p)),
                  pl.BlockSpec((1, seq, LANES), lambda b, p, i: (b, 0, p)),
                  pl.BlockSpec((1, seq, LANES), lambda b, p, i: (b, 0, p)),
                  pl.BlockSpec((1, seq, LANES), lambda b, p, i: (b, 0, 0))],
        out_specs=pl.BlockSpec((1, tq, LANES), lambda b, p, i: (b, i, p)),
        out_shape=jax.ShapeDtypeStruct((bsz, seq, dm), BF16),
        scratch_shapes=[pltpu.VMEM((seq, 2 * LANES), BF16), pltpu.VMEM((2, tq, 2 * LANES), BF16),
                        pltpu.VMEM((2 * tq, tq), F32), pltpu.VMEM((2 * tq, tq), F32),
                        pltpu.VMEM((tq, LANES), F32), pltpu.VMEM((tq, LANES), F32),
                        pltpu.VMEM((tq, LANES), F32), pltpu.VMEM((tq, LANES), F32)],
        compiler_params=_cparams("arbitrary", "arbitrary", "arbitrary"),
        name="fox_prompt",
    )(q, kb, vb, fc)


def _fox_sample_kernel(n_group, n_new, pt_ref, q_ref, kn_ref, vn_ref, lfn_ref, suf_ref, *refs):
    k_refs = refs[0:n_group]
    v_refs = refs[n_group:2 * n_group]
    lf_refs = refs[2 * n_group:3 * n_group]
    o_ref, m_scr, l_scr, acc_scr, fn_scr, carry_scr = refs[3 * n_group:]
    i = pl.program_id(1)
    n_heads, n_rows, dh = q_ref.shape[1:]
    page = suf_ref.shape[0]

    @pl.when(i == 0)
    def _():
        y = lfn_ref[0]
        t3 = lax.broadcasted_iota(jnp.int32, y.shape, 1)
        sh = 1
        while sh < n_new:
            y = y + jnp.where(t3 >= sh, pltpu.roll(y, sh, 1), 0.0)
            sh *= 2
        fn_scr[...] = y
        q = q_ref[0]
        t1 = lax.broadcasted_iota(jnp.int32, (n_heads, n_rows, 1), 1)
        fcol = y[:, :, 0:1]
        ss = []
        for u in range(n_new):
            s_u = jnp.sum(q * kn_ref[0, :, u:u + 1, :], axis=2, keepdims=True) + fcol - fcol[:, u:u + 1, :]
            ss.append(jnp.where(t1 >= u, s_u, -jnp.inf))
        m0 = functools.reduce(jnp.maximum, ss)
        l0 = jnp.zeros_like(m0)
        acc0 = jnp.zeros((n_heads, n_rows, dh), F32)
        for u in range(n_new):
            p_u = jnp.exp(ss[u] - m0)
            l0 = l0 + p_u
            acc0 = acc0 + p_u * vn_ref[0, :, u:u + 1, :]
        m_scr[...] = jnp.broadcast_to(m0, m_scr.shape)
        l_scr[...] = jnp.broadcast_to(l0, l_scr.shape)
        acc_scr[...] = acc0
        carry_scr[...] = jnp.zeros_like(carry_scr)

    suf = suf_ref[...]
    withins, prefixes = [], []
    run = carry_scr[:, 0:1]
    for g in range(n_group):
        lf_t = lf_refs[g][0]
        withins.append(_dot3_x_sel(lf_t, suf))
        prefixes.append(run)
        run = run + jnp.sum(lf_t, axis=1, keepdims=True)
    carry_scr[...] = jnp.broadcast_to(run, carry_scr.shape)

    shift = jnp.concatenate([withins[g] + prefixes[g] for g in range(n_group)], axis=1)
    fn = jnp.concatenate([fn_scr[...]] * n_group, axis=2)
    s3 = jnp.stack([
        _dot(q_ref[0, h].astype(BF16),
             jnp.concatenate([k_refs[g][0, h] for g in range(n_group)], axis=1).astype(BF16)) + shift[h:h + 1, :]
        for h in range(n_heads)]) + fn
    m_old = m_scr[...]
    m_new = jnp.maximum(m_old, jnp.max(s3, axis=2, keepdims=True))
    alpha = jnp.exp(m_old - m_new)
    p3 = jnp.exp(s3 - jnp.concatenate([m_new] * n_group, axis=2))
    l_scr[...] = alpha * l_scr[...] + jnp.sum(p3, axis=2, keepdims=True)
    m_scr[...] = m_new
    pb = p3.astype(BF16)
    pv = jnp.stack([
        _dot_nt(pb[h], jnp.concatenate([v_refs[g][0, h] for g in range(n_group)], axis=1).astype(BF16))
        for h in range(n_heads)])
    acc_scr[...] = alpha[:, :, 0:dh] * acc_scr[...] + pv

    @pl.when(i == pl.num_programs(1) - 1)
    def _():
        o_ref[0] = acc_scr[...] / l_scr[:, :, 0:dh]


def _fox_sample(q4, kn4, vn4, lfn4, cache_k, cache_v, cache_lf_t, page_table, suf, n_group, n_new):
    db, n_heads, n_rows, dh = q4.shape
    n_pages = page_table.shape[1]
    page = cache_lf_t.shape[2]
    assert n_pages % n_group == 0 and page == LANES and n_new <= n_rows
    pt_flat = page_table.reshape(-1)

    def page_map(g, ndim):
        def imap(b, i, pt):
            return (pt[b * n_pages + (n_pages - 1 - (i * n_group + g))],) + (0,) * (ndim - 1)
        return imap

    seq_spec = lambda c: pl.BlockSpec((1, n_heads, n_rows, c), lambda b, i, pt: (b, 0, 0, 0))
    in_specs = [seq_spec(dh), seq_spec(dh), seq_spec(dh), seq_spec(LANES),
                pl.BlockSpec(suf.shape, lambda b, i, pt: (0, 0))]
    in_specs += [pl.BlockSpec((1, n_heads, dh, page), page_map(g, 4)) for g in range(n_group)]
    in_specs += [pl.BlockSpec((1, n_heads, dh, page), page_map(g, 4)) for g in range(n_group)]
    in_specs += [pl.BlockSpec((1, n_heads, page), page_map(g, 3)) for g in range(n_group)]
    grid_spec = pltpu.PrefetchScalarGridSpec(
        num_scalar_prefetch=1,
        grid=(db, n_pages // n_group),
        in_specs=in_specs,
        out_specs=seq_spec(dh),
        scratch_shapes=[pltpu.VMEM((n_heads, n_rows, LANES), F32), pltpu.VMEM((n_heads, n_rows, LANES), F32),
                        pltpu.VMEM((n_heads, n_rows, dh), F32), pltpu.VMEM((n_heads, n_rows, LANES), F32),
                        pltpu.VMEM((n_heads, LANES), F32)],
    )
    return pl.pallas_call(
        functools.partial(_fox_sample_kernel, n_group, n_new),
        grid_spec=grid_spec,
        out_shape=jax.ShapeDtypeStruct((db, n_heads, n_rows, dh), F32),
        compiler_params=_cparams("arbitrary", "arbitrary"),
        name="fox_sample",
    )(pt_flat, q4, kn4, vn4, lfn4, suf, *([cache_k] * n_group), *([cache_v] * n_group),
      *([cache_lf_t] * n_group))


def _prep_weights(norm_w, m_w_in, m_conv_w, m_conv_b, m_dt_bias, m_a_log, m_d, m_gnorm_w, m_w_out,
                  ffn_w_in, ffn_conv_w, ffn_conv_b, ffn_w_out, kv_norm_w, w_kvf, b_f, w_q, w_o):
    d_model = norm_w.shape[-1]
    d_inner = m_w_out.shape[1]
    n_sheads = m_dt_bias.shape[1]
    conv_dim = m_conv_w.shape[2]
    d_ff = ffn_w_out.shape[1]
    att_dim = w_q.shape[2]
    n_aheads = b_f.shape[0]
    pad_lanes = lambda a: jnp.pad(a, [(0, 0)] * (a.ndim - 1) + [(0, LANES - a.shape[-1])])
    w = {}
    w["norm"] = norm_w.reshape(norm_w.shape[0], norm_w.shape[1], 1, d_model)
    w["m_wz"] = m_w_in[:, :, :d_inner].astype(BF16)
    w["m_wx"] = m_w_in[:, :, d_inner:d_inner + conv_dim].astype(BF16)
    w["m_wdt"] = pad_lanes(m_w_in[:, :, d_inner + conv_dim:]).astype(BF16)
    w["m_conv_w"] = m_conv_w
    w["m_conv_b"] = m_conv_b[:, None, :]
    w["m_dtb"] = pad_lanes(m_dt_bias)[:, None, :]
    w["m_alog"] = pad_lanes(m_a_log)[:, None, :]
    w["m_dsk"] = jnp.repeat(m_d.astype(F32), SSM_HEAD_DIM, axis=1)[:, None, :]
    w["m_gw"] = m_gnorm_w[:, None, :]
    w["m_wout"] = m_w_out.astype(BF16)
    w["f_wg"] = ffn_w_in[:, :, :d_ff].astype(BF16)
    w["f_wu"] = ffn_w_in[:, :, d_ff:].astype(BF16)
    w["f_conv_w"] = ffn_conv_w
    w["f_conv_b"] = ffn_conv_b[:, None, :]
    w["f_wout"] = ffn_w_out.astype(BF16)
    w["kv_norm"] = kv_norm_w[None, :]
    w["wk"] = w_kvf[:, :att_dim].astype(BF16)
    w["wv"] = w_kvf[:, att_dim:2 * att_dim].astype(BF16)
    w["wf"] = pad_lanes(w_kvf[:, 2 * att_dim:]).astype(BF16)
    w["bf"] = pad_lanes(b_f[None, :])
    w["wq"] = w_q.astype(BF16)
    w["wo"] = w_o.astype(BF16)
    hh = jnp.arange(LANES)[:, None]
    cc = jnp.arange(d_inner)[None, :] // SSM_HEAD_DIM
    w["ex"] = (hh == cc).astype(BF16)
    r = jnp.arange(SSD_CHUNK)
    w["tri_chunk"] = (r[:, None] >= r[None, :]).astype(BF16)
    p = jnp.arange(PAGE_SIZE)
    w["suf"] = (p[:, None] > p[None, :]).astype(BF16)
    w["n_sheads"] = n_sheads
    w["n_aheads"] = n_aheads
    return w


TM_SEQ = 512
TM_MAMBA_IN = 256
TQ = 512
N_PAGE_GROUP = 8


def _tri(n):
    r = jnp.arange(n)
    return (r[:, None] >= r[None, :]).astype(BF16)


def _prompt_trunk(x, w, depth, n_a):
    bsz, seq, d = x.shape
    m = bsz * seq
    h = x
    d_inner = w["m_wout"].shape[1]
    outs_ssm, outs_mconv, outs_ffn = [], [], []
    kf = vf = kb = vb = lf = fc = None
    for i in range(depth):
        nw = w["norm"][i]
        if i < n_a:
            z, xa, dt, cst = _mamba_in_seq(h, nw[0], w["m_wz"][i], w["m_wx"][i], w["m_wdt"][i],
                                           w["m_conv_w"][i], w["m_conv_b"][i], w["m_dtb"][i], TM_MAMBA_IN)
            s0 = jnp.zeros((bsz, d_inner, D_STATE), F32)
            y, s_new = _ssd(xa, z, dt, s0, w["m_alog"][i], w["m_dsk"][i], w["ex"], w["tri_chunk"], SSD_CHUNK)
            h = _out_proj(y.reshape(m, d_inner), w["m_wout"][i], h.reshape(m, d), nw[1], TM_SEQ,
                          gw=w["m_gw"][i]).reshape(bsz, seq, d)
            width = w["m_conv_w"].shape[1]
            outs_mconv.append(cst[:, SUBLANES - (width - 1):, :])
            outs_ssm.append(s_new.reshape(bsz, w["n_sheads"], SSM_HEAD_DIM, D_STATE))
        else:
            j = i - n_a
            q = _q_proj(h.reshape(m, d), nw[0], w["wq"][j], ATT_HEAD_DIM ** -0.5 * LOG2E, TM_SEQ, BF16)
            o = _fox_prompt(q.reshape(bsz, seq, -1), kb, vb, fc, TQ)
            h = _out_proj(o.reshape(m, -1), w["wo"][j], h.reshape(m, d), nw[1], TM_SEQ).reshape(bsz, seq, d)
        act, fst = _ffn_in_seq(h, nw[2], w["f_wg"][i], w["f_wu"][i], w["f_conv_w"][i], w["f_conv_b"][i], TM_SEQ)
        h = _out_proj(act.reshape(m, -1), w["f_wout"][i], h.reshape(m, d), nw[3], TM_SEQ).reshape(bsz, seq, d)
        fwidth = w["f_conv_w"].shape[1]
        outs_ffn.append(fst[:, SUBLANES - (fwidth - 1):, :])
        if i == n_a - 1:
            kf, vf, kb, vb, lf, fc = _kvf(h, w["kv_norm"], w["wk"], w["wv"], w["wf"], w["bf"], _tri(TM_SEQ), TM_SEQ)
    n_ah = w["n_aheads"]
    return (h, jnp.stack(outs_ssm), jnp.stack(outs_mconv), jnp.stack(outs_ffn),
            kf.reshape(bsz, seq, n_ah, ATT_HEAD_DIM), vf.reshape(bsz, seq, n_ah, ATT_HEAD_DIM), lf[:, :, :n_ah])


def _to_tm(a):
    db, n_new, c = a.shape
    return jnp.swapaxes(a, 0, 1).reshape(n_new * db, c)


def _from_tm(a, db):
    m, c = a.shape
    return jnp.swapaxes(a.reshape(m // db, db, c), 0, 1)


def _heads_major(a, n_heads):
    db, n_new, c = a.shape
    a4 = jnp.swapaxes(a.reshape(db, n_new, n_heads, c // n_heads), 1, 2)
    return jnp.pad(a4, ((0, 0), (0, 0), (0, SUBLANES - n_new), (0, 0)))


def _sample_trunk(x, ssm_st, mconv_st, ffn_st, cache_k, cache_v, cache_logf, page_table, w, depth, n_a):
    db, n_new, d = x.shape
    m = db * n_new
    h = _to_tm(x)
    d_inner = w["m_wout"].shape[1]
    n_ah = w["n_aheads"]
    outs_ssm, outs_mconv, outs_ffn = [], [], []
    kf = vf = lf = None
    pad_t = lambda a: jnp.pad(a, ((0, 0), (0, SSD_CHUNK - n_new), (0, 0)))
    for i in range(depth):
        nw = w["norm"][i]
        if i < n_a:
            st_tm = jnp.swapaxes(mconv_st[i], 0, 1)
            z, xa, dt, nst = _mamba_in_step(h, st_tm, nw[0], w["m_wz"][i], w["m_wx"][i], w["m_wdt"][i],
                                            w["m_conv_w"][i], w["m_conv_b"][i], w["m_dtb"][i], n_new)
            s0 = ssm_st[i].reshape(db, d_inner, D_STATE)
            y, s_new = _ssd(pad_t(_from_tm(xa, db)), pad_t(_from_tm(z, db)), pad_t(_from_tm(dt, db)), s0,
                            w["m_alog"][i], w["m_dsk"][i], w["ex"], w["tri_chunk"], n_new)
            y_tm = _to_tm(y[:, :n_new, :])
            h = _out_proj(y_tm, w["m_wout"][i], h, nw[1], m, gw=w["m_gw"][i])
            outs_mconv.append(jnp.swapaxes(nst, 0, 1))
            outs_ssm.append(s_new.reshape(db, w["n_sheads"], SSM_HEAD_DIM, D_STATE))
        else:
            j = i - n_a
            q = _from_tm(_q_proj(h, nw[0], w["wq"][j], ATT_HEAD_DIM ** -0.5, m, F32), db)
            ck_t = jnp.transpose(cache_k, (0, 2, 3, 1))
            cv_t = jnp.transpose(cache_v, (0, 2, 3, 1))
            clf_t = jnp.swapaxes(cache_logf, 1, 2)
            lfn3 = jnp.pad(jnp.swapaxes(lf[:, :, :n_ah], 1, 2), ((0, 0), (0, 0), (0, SUBLANES - n_new)))
            lfn4 = jnp.broadcast_to(lfn3[..., None], (db, n_ah, SUBLANES, LANES))
            o4 = _fox_sample(_heads_major(q, n_ah), _heads_major(kf, n_ah), _heads_major(vf, n_ah), lfn4,
                             ck_t, cv_t, clf_t, page_table, w["suf"], N_PAGE_GROUP, n_new)
            o = jnp.swapaxes(o4[:, :, :n_new, :], 1, 2).reshape(db, n_new, -1)
            h = _out_proj(_to_tm(o), w["wo"][j], h, nw[1], m)
        fst_tm = jnp.swapaxes(ffn_st[i], 0, 1)
        act, nfst = _ffn_in_step(h, fst_tm, nw[2], w["f_wg"][i], w["f_wu"][i], w["f_conv_w"][i],
                                 w["f_conv_b"][i], n_new)
        h = _out_proj(act, w["f_wout"][i], h, nw[3], m)
        outs_ffn.append(jnp.swapaxes(nfst, 0, 1))
        if i == n_a - 1:
            kf_tm, vf_tm, _, _, lf_tm, _ = _kvf(h[None], w["kv_norm"], w["wk"], w["wv"], w["wf"], w["bf"],
                                                _tri(m), m)
            kf, vf, lf = _from_tm(kf_tm[0], db), _from_tm(vf_tm[0], db), _from_tm(lf_tm[0], db)
    return (_from_tm(h, db), jnp.stack(outs_ssm), jnp.stack(outs_mconv), jnp.stack(outs_ffn),
            kf.reshape(db, n_new, n_ah, ATT_HEAD_DIM), vf.reshape(db, n_new, n_ah, ATT_HEAD_DIM), lf[:, :, :n_ah])


def kernel(x_prompt, x_sample, state_ssm, state_mconv, state_ffn, cache_k, cache_v, cache_logf, page_table,
           norm_w, m_w_in, m_conv_w, m_conv_b, m_dt_bias, m_a_log, m_d, m_gnorm_w, m_w_out,
           ffn_w_in, ffn_conv_w, ffn_conv_b, ffn_w_out, kv_norm_w, w_kvf, b_f, w_q, w_o):
    depth = norm_w.shape[0]
    n_a = m_w_in.shape[0]
    w = _prep_weights(norm_w, m_w_in, m_conv_w, m_conv_b, m_dt_bias, m_a_log, m_d, m_gnorm_w, m_w_out,
                      ffn_w_in, ffn_conv_w, ffn_conv_b, ffn_w_out, kv_norm_w, w_kvf, b_f, w_q, w_o)
    y_p, p_ssm, p_mconv, p_ffn, p_k, p_v, p_lf = _prompt_trunk(x_prompt, w, depth, n_a)
    y_s, s_ssm, s_mconv, s_ffn, s_k, s_v, s_lf = _sample_trunk(
        x_sample, state_ssm, state_mconv, state_ffn, cache_k, cache_v, cache_logf, page_table, w, depth, n_a)
    return (y_p, y_s, p_ssm, p_mconv, p_ffn, p_k, p_v, p_lf, s_ssm, s_mconv, s_ffn, s_k, s_v, s_lf)
```

```python
import functools
import math

import jax
import jax.numpy as jnp
from jax import lax
from jax.experimental import pallas as pl
from jax.experimental.pallas import tpu as pltpu

F32 = jnp.float32
BF16 = jnp.bfloat16
EPS = 1e-6

LANES = 128
SUBLANES = 8
VMEM_LIMIT_BYTES = 56 * 1024 * 1024

SSM_HEAD_DIM = 64
SSM_GROUPS = 4
D_STATE = 128
SSD_CHUNK = 128
ATT_HEAD_DIM = 64
PAGE_SIZE = 128


def _cparams(*sem):
    return pltpu.CompilerParams(dimension_semantics=sem, vmem_limit_bytes=VMEM_LIMIT_BYTES)


def _const_spec(shape):
    nd = len(shape)
    return pl.BlockSpec(shape, lambda *_: (0,) * nd)


def _rms(x, w):
    ms = jnp.mean(x * x, axis=-1, keepdims=True)
    return x * lax.rsqrt(ms + EPS) * w


def _silu(x):
    return x / (1.0 + jnp.exp(-x))


def _softplus(x):
    return jnp.maximum(x, 0.0) + jnp.log1p(jnp.exp(-jnp.abs(x)))


def _dot(a, b):
    return jnp.dot(a, b, preferred_element_type=F32)


def _dot_nt(a, b):
    return lax.dot_general(a, b, (((1,), (1,)), ((), ())), preferred_element_type=F32)


def _dot_tn(a, b):
    return lax.dot_general(a, b, (((0,), (0,)), ((), ())), preferred_element_type=F32)


def _split3(x):
    hi = x.astype(BF16)
    r1 = x - hi.astype(F32)
    mid = r1.astype(BF16)
    lo = (r1 - mid.astype(F32)).astype(BF16)
    return hi, mid, lo


def _dot3_x_sel(x, sel):
    hi, mid, lo = _split3(x)
    return _dot(hi, sel) + _dot(mid, sel) + _dot(lo, sel)


def _dot3_sel_x(sel, x):
    hi, mid, lo = _split3(x)
    return _dot(sel, hi) + _dot(sel, mid) + _dot(sel, lo)


def _conv_rows(g, carry8, w, b, gbuf):
    tm, n = g.shape
    width = w.shape[0]
    gbuf[0:SUBLANES, 0:n] = carry8
    gbuf[SUBLANES:SUBLANES + tm, 0:n] = g
    out = b + g * w[width - 1:width]
    for k in range(width - 1):
        start = SUBLANES - (width - 1 - k)
        out = out + gbuf[start:start + tm, 0:n] * w[k:k + 1]
    return out


def _conv_slabs(slabs, w, b, n_new):
    width = w.shape[0]
    outs = []
    for t in range(n_new):
        o = b + slabs[t] * w[0:1]
        for k in range(1, width):
            o = o + slabs[t + k] * w[k:k + 1]
        outs.append(o)
    return outs


COL_CHUNK = 512


def _col_chunks(n, step=COL_CHUNK):
    out, c = [], 0
    while c < n:
        w = min(step, n - c)
        out.append((c, w))
        c += w
    return out


def _mamba_in_seq_kernel(x_ref, nw_ref, wz_ref, wx_ref, wdt_ref, cw_ref, cb_ref, dtb_ref,
                         z_ref, xa_ref, dt_ref, st_ref, xn_scr, carry_scr, gbuf_scr):
    li = pl.program_id(1)

    @pl.when(li == 0)
    def _():
        carry_scr[...] = jnp.zeros_like(carry_scr)

    xn_scr[...] = _rms(x_ref[0], nw_ref[...]).astype(BF16)
    xn = xn_scr[...]
    tm = xn.shape[0]
    for c0, cw in _col_chunks(wz_ref.shape[1]):
        z_ref[0, :, c0:c0 + cw] = _dot(xn, wz_ref[:, c0:c0 + cw])
    for c0, cw in _col_chunks(wx_ref.shape[1]):
        g = _dot(xn, wx_ref[:, c0:c0 + cw])
        carry8 = carry_scr[:, c0:c0 + cw]
        y = _conv_rows(g, carry8, cw_ref[:, c0:c0 + cw], cb_ref[:, c0:c0 + cw], gbuf_scr)
        xa_ref[0, :, c0:c0 + cw] = _silu(y)
        carry_scr[:, c0:c0 + cw] = g[tm - SUBLANES:]
    dt_ref[0] = _softplus(_dot(xn, wdt_ref[...]) + dtb_ref[...])
    st_ref[0] = carry_scr[...]


def _mamba_in_seq(h, nw, wz, wx, wdt, conv_w, conv_b, dtb, tm):
    bsz, seq, d = h.shape
    nz, nx = wz.shape[1], wx.shape[1]
    assert seq % tm == 0
    grid = (bsz, seq // tm)
    row = lambda n: pl.BlockSpec((1, tm, n), lambda b, l: (b, l, 0))
    return pl.pallas_call(
        _mamba_in_seq_kernel,
        grid=grid,
        in_specs=[row(d), _const_spec(nw.shape), _const_spec(wz.shape), _const_spec(wx.shape),
                  _const_spec(wdt.shape), _const_spec(conv_w.shape), _const_spec(conv_b.shape),
                  _const_spec(dtb.shape)],
        out_specs=[row(nz), row(nx), row(LANES),
                   pl.BlockSpec((1, SUBLANES, nx), lambda b, l: (b, 0, 0))],
        out_shape=[jax.ShapeDtypeStruct((bsz, seq, nz), F32),
                   jax.ShapeDtypeStruct((bsz, seq, nx), F32),
                   jax.ShapeDtypeStruct((bsz, seq, LANES), F32),
                   jax.ShapeDtypeStruct((bsz, SUBLANES, nx), F32)],
        scratch_shapes=[pltpu.VMEM((tm, d), BF16), pltpu.VMEM((SUBLANES, nx), F32),
                        pltpu.VMEM((SUBLANES + tm, COL_CHUNK), F32)],
        compiler_params=_cparams("arbitrary", "arbitrary"),
        name="mamba_in_seq",
    )(h, nw, wz, wx, wdt, conv_w, conv_b, dtb)


def _mamba_in_step_kernel(n_new, x_ref, st_ref, nw_ref, wz_ref, wx_ref, wdt_ref, cw_ref, cb_ref, dtb_ref,
                          z_ref, xa_ref, dt_ref, nst_ref):
    xn = _rms(x_ref[...], nw_ref[...]).astype(BF16)
    nb = xn.shape[0] // n_new
    width = cw_ref.shape[0]
    for c0, cw in _col_chunks(wz_ref.shape[1]):
        z_ref[:, c0:c0 + cw] = _dot(xn, wz_ref[:, c0:c0 + cw])
    for c0, cw in _col_chunks(wx_ref.shape[1]):
        g = _dot(xn, wx_ref[:, c0:c0 + cw])
        slabs = [st_ref[k, :, c0:c0 + cw] for k in range(width - 1)]
        slabs += [g[t * nb:(t + 1) * nb] for t in range(n_new)]
        outs = _conv_slabs(slabs, cw_ref[:, c0:c0 + cw], cb_ref[:, c0:c0 + cw], n_new)
        for t in range(n_new):
            xa_ref[t * nb:(t + 1) * nb, c0:c0 + cw] = _silu(outs[t])
        for k in range(width - 1):
            nst_ref[k, :, c0:c0 + cw] = slabs[n_new + k]
    dt_ref[...] = _softplus(_dot(xn, wdt_ref[...]) + dtb_ref[...])


def _mamba_in_step(x_tm, st_tm, nw, wz, wx, wdt, conv_w, conv_b, dtb, n_new):
    m, d = x_tm.shape
    nz, nx = wz.shape[1], wx.shape[1]
    args = (x_tm, st_tm, nw, wz, wx, wdt, conv_w, conv_b, dtb)
    return pl.pallas_call(
        functools.partial(_mamba_in_step_kernel, n_new),
        grid=(1,),
        in_specs=[_const_spec(a.shape) for a in args],
        out_specs=[_const_spec((m, nz)), _const_spec((m, nx)), _const_spec((m, LANES)),
                   _const_spec(st_tm.shape)],
        out_shape=[jax.ShapeDtypeStruct((m, nz), F32), jax.ShapeDtypeStruct((m, nx), F32),
                   jax.ShapeDtypeStruct((m, LANES), F32), jax.ShapeDtypeStruct(st_tm.shape, F32)],
        compiler_params=_cparams("arbitrary"),
        name="mamba_in_step",
    )(*args)


def _ssd_kernel(valid, n_sub, xa_ref, z_ref, dt_ref, s0_ref, alog_ref, dsk_ref, ex_ref, tri_ref,
                y_ref, sout_ref, st_scr, *pad_scrs):
    ci = pl.program_id(1)
    d_inner = z_ref.shape[2]
    if pad_scrs:
        t_rows = pad_scrs[0].shape[0]

        @pl.when((pl.program_id(0) == 0) & (ci == 0))
        def _():
            for p in pad_scrs:
                p[...] = jnp.zeros_like(p)

        for p, r in zip(pad_scrs, (xa_ref, z_ref, dt_ref)):
            p[0:valid, :] = r[0]
        xa_at = lambda rs, cs: pad_scrs[0][rs, cs]
        z_at = lambda rs, cs: pad_scrs[1][rs, cs]
        dt_at = lambda rs: pad_scrs[2][rs, :]
    else:
        t_rows = xa_ref.shape[1] // n_sub
        xa_at = lambda rs, cs: xa_ref[0, rs, cs]
        z_at = lambda rs, cs: z_ref[0, rs, cs]
        dt_at = lambda rs: dt_ref[0, rs, :]
    gn = SSM_GROUPS * D_STATE
    n_heads = d_inner // SSM_HEAD_DIM
    heads_per_group = n_heads // SSM_GROUPS
    gcols = heads_per_group * SSM_HEAD_DIM

    @pl.when(ci == 0)
    def _():
        for c0 in range(0, d_inner, LANES):
            st_scr[:, c0:c0 + LANES] = s0_ref[0, c0:c0 + LANES, :].T

    li = lax.broadcasted_iota(jnp.int32, (t_rows, t_rows), 0)
    si = lax.broadcasted_iota(jnp.int32, (t_rows, t_rows), 1)
    causal = li >= si
    lane = lax.broadcasted_iota(jnp.int32, (t_rows, LANES), 1)
    first_head = lane < SSM_HEAD_DIM
    ex = ex_ref[...]
    a_row = -jnp.exp(alog_ref[...])

    for sub in range(n_sub):
        rs = slice(sub * t_rows, (sub + 1) * t_rows)
        xs = xa_at(rs, slice(0, d_inner))
        bm = xa_at(rs, slice(d_inner, d_inner + gn)).astype(BF16)
        cm = xa_at(rs, slice(d_inner + gn, d_inner + 2 * gn)).astype(BF16)
        dt = dt_at(rs)
        if valid < t_rows:
            dt = jnp.where(lax.broadcasted_iota(jnp.int32, dt.shape, 0) < valid, dt, 0.0)
        dta = dt * a_row
        acum = _dot3_sel_x(tri_ref[...], dta)
        acum_t = acum.T
        dt_e = _dot3_x_sel(dt, ex)
        acum_e = _dot3_x_sel(acum, ex)
        xdt = xs * dt_e
        xdt_b = xdt.astype(BF16)
        last_e = acum_e[t_rows - 1:t_rows, :]
        e_acum = jnp.exp(acum_e)
        xde_b = (xdt * jnp.exp(last_e - acum_e)).astype(BF16)
        e_last = jnp.exp(last_e)

        for g in range(SSM_GROUPS):
            c_g = cm[:, g * D_STATE:(g + 1) * D_STATE]
            b_g = bm[:, g * D_STATE:(g + 1) * D_STATE]
            cb = _dot_nt(c_g, b_g)
            g0 = g * gcols
            pieces = []
            for k in range(heads_per_group // 2):
                h1 = g * heads_per_group + 2 * k
                ms = []
                for hh in (h1, h1 + 1):
                    seg = acum[:, hh:hh + 1] - acum_t[hh:hh + 1, :]
                    dec = jnp.exp(jnp.where(causal, seg, -jnp.inf))
                    ms.append((cb * dec).astype(BF16))
                x2 = xdt_b[:, g0 + k * LANES:g0 + (k + 1) * LANES]
                zero = jnp.zeros_like(x2)
                rhs = jnp.concatenate([jnp.where(first_head, x2, zero), jnp.where(first_head, zero, x2)], axis=0)
                pieces.append(_dot(jnp.concatenate(ms, axis=1), rhs))
            y_diag = jnp.concatenate(pieces, axis=1)
            st_g = st_scr[:, g0:g0 + gcols]
            y_off = _dot(c_g, st_g.astype(BF16)) * e_acum[:, g0:g0 + gcols]
            xs_g = xs[:, g0:g0 + gcols]
            y = y_diag + y_off + dsk_ref[:, g0:g0 + gcols] * xs_g
            gated = y * _silu(z_at(rs, slice(g0, g0 + gcols)))
            if pad_scrs:
                y_ref[0, :, g0:g0 + gcols] = gated[0:valid]
            else:
                y_ref[0, rs, g0:g0 + gcols] = gated
            st_scr[:, g0:g0 + gcols] = e_last[:, g0:g0 + gcols] * st_g + _dot_tn(b_g, xde_b[:, g0:g0 + gcols])

    @pl.when(ci == pl.num_programs(1) - 1)
    def _():
        for c0 in range(0, d_inner, LANES):
            sout_ref[0, c0:c0 + LANES, :] = st_scr[:, c0:c0 + LANES].T


def _ssd(xa, z, dt, s0, a_row, dsk_e, ex, tri, n_sub):
    bsz, seq, nx = xa.shape
    d_inner = z.shape[2]
    short = seq < SSD_CHUNK
    rows = seq if short else SSD_CHUNK * n_sub
    assert seq % rows == 0 and (n_sub == 1 or not short)
    row = lambda n: pl.BlockSpec((1, rows, n), lambda b, c: (b, c, 0))
    st_spec = pl.BlockSpec((1, d_inner, D_STATE), lambda b, c: (b, 0, 0))
    consts = (a_row, dsk_e, ex, tri)
    scratch = [pltpu.VMEM((D_STATE, d_inner), F32)]
    if short:
        scratch += [pltpu.VMEM((SSD_CHUNK, n), F32) for n in (nx, d_inner, LANES)]
    return pl.pallas_call(
        functools.partial(_ssd_kernel, seq if short else SSD_CHUNK, n_sub),
        grid=(bsz, seq // rows),
        in_specs=[row(nx), row(d_inner), row(LANES), st_spec] + [_const_spec(c.shape) for c in consts],
        out_specs=[row(d_inner), st_spec],
        out_shape=[jax.ShapeDtypeStruct((bsz, seq, d_inner), F32),
                   jax.ShapeDtypeStruct((bsz, d_inner, D_STATE), F32)],
        scratch_shapes=scratch,
        compiler_params=_cparams("arbitrary", "arbitrary"),
        name="ssd_chunk",
    )(xa, z, dt, s0, *consts)


def _out_proj_kernel(pre_norm, a_ref, *refs):
    if pre_norm:
        gw_ref, w_ref, h_ref, pw_ref, o_ref = refs
        a = _rms(a_ref[...].astype(F32), gw_ref[...]).astype(BF16)
    else:
        w_ref, h_ref, pw_ref, o_ref = refs
        a = a_ref[...].astype(BF16)
    o_ref[...] = h_ref[...] + _rms(_dot(a, w_ref[...]), pw_ref[...])


def _out_proj(a, w, h, pw, tm, gw=None):
    m, k = a.shape
    d = w.shape[1]
    assert m % tm == 0
    row = lambda n: pl.BlockSpec((tm, n), lambda i: (i, 0))
    ins = [a] + ([gw] if gw is not None else []) + [w, h, pw]
    specs = [row(k)] + ([_const_spec(gw.shape)] if gw is not None else []) + \
        [_const_spec(w.shape), row(d), _const_spec(pw.shape)]
    return pl.pallas_call(
        functools.partial(_out_proj_kernel, gw is not None),
        grid=(m // tm,),
        in_specs=specs,
        out_specs=row(d),
        out_shape=jax.ShapeDtypeStruct((m, d), F32),
        compiler_params=_cparams("arbitrary"),
        name="out_proj",
    )(*ins)


def _ffn_in_seq_kernel(x_ref, nw_ref, wg_ref, wu_ref, cw_ref, cb_ref, act_ref, st_ref, xn_scr, carry_scr, gbuf_scr):
    li = pl.program_id(1)

    @pl.when(li == 0)
    def _():
        carry_scr[...] = jnp.zeros_like(carry_scr)

    xn_scr[...] = _rms(x_ref[0], nw_ref[...]).astype(BF16)
    xn = xn_scr[...]
    tm = xn.shape[0]
    for c0, cw in _col_chunks(wg_ref.shape[1]):
        g = _dot(xn, wg_ref[:, c0:c0 + cw])
        up = _dot(xn, wu_ref[:, c0:c0 + cw])
        y = _conv_rows(g, carry_scr[:, c0:c0 + cw], cw_ref[:, c0:c0 + cw], cb_ref[:, c0:c0 + cw], gbuf_scr)
        act_ref[0, :, c0:c0 + cw] = (_silu(y) * up).astype(act_ref.dtype)
        carry_scr[:, c0:c0 + cw] = g[tm - SUBLANES:]
    st_ref[0] = carry_scr[...]


def _ffn_in_seq(h, nw, wg, wu, conv_w, conv_b, tm):
    bsz, seq, d = h.shape
    nf = wg.shape[1]
    assert seq % tm == 0
    row = lambda n: pl.BlockSpec((1, tm, n), lambda b, l: (b, l, 0))
    return pl.pallas_call(
        _ffn_in_seq_kernel,
        grid=(bsz, seq // tm),
        in_specs=[row(d), _const_spec(nw.shape), _const_spec(wg.shape), _const_spec(wu.shape),
                  _const_spec(conv_w.shape), _const_spec(conv_b.shape)],
        out_specs=[row(nf), pl.BlockSpec((1, SUBLANES, nf), lambda b, l: (b, 0, 0))],
        out_shape=[jax.ShapeDtypeStruct((bsz, seq, nf), BF16),
                   jax.ShapeDtypeStruct((bsz, SUBLANES, nf), F32)],
        scratch_shapes=[pltpu.VMEM((tm, d), BF16), pltpu.VMEM((SUBLANES, nf), F32),
                        pltpu.VMEM((SUBLANES + tm, COL_CHUNK), F32)],
        compiler_params=_cparams("arbitrary", "arbitrary"),
        name="ffn_in_seq",
    )(h, nw, wg, wu, conv_w, conv_b)


def _ffn_in_step_kernel(n_new, x_ref, st_ref, nw_ref, wg_ref, wu_ref, cw_ref, cb_ref, act_ref, nst_ref):
    xn = _rms(x_ref[...], nw_ref[...]).astype(BF16)
    nb = xn.shape[0] // n_new
    width = cw_ref.shape[0]
    for c0, cw in _col_chunks(wg_ref.shape[1]):
        g = _dot(xn, wg_ref[:, c0:c0 + cw])
        up = _dot(xn, wu_ref[:, c0:c0 + cw])
        slabs = [st_ref[k, :, c0:c0 + cw] for k in range(width - 1)]
        slabs += [g[t * nb:(t + 1) * nb] for t in range(n_new)]
        outs = _conv_slabs(slabs, cw_ref[:, c0:c0 + cw], cb_ref[:, c0:c0 + cw], n_new)
        for t in range(n_new):
            act_ref[t * nb:(t + 1) * nb, c0:c0 + cw] = \
                (_silu(outs[t]) * up[t * nb:(t + 1) * nb]).astype(act_ref.dtype)
        for k in range(width - 1):
            nst_ref[k, :, c0:c0 + cw] = slabs[n_new + k]


def _ffn_in_step(x_tm, st_tm, nw, wg, wu, conv_w, conv_b, n_new):
    m, d = x_tm.shape
    nf = wg.shape[1]
    args = (x_tm, st_tm, nw, wg, wu, conv_w, conv_b)
    return pl.pallas_call(
        functools.partial(_ffn_in_step_kernel, n_new),
        grid=(1,),
        in_specs=[_const_spec(a.shape) for a in args],
        out_specs=[_const_spec((m, nf)), _const_spec(st_tm.shape)],
        out_shape=[jax.ShapeDtypeStruct((m, nf), BF16), jax.ShapeDtypeStruct(st_tm.shape, F32)],
        compiler_params=_cparams("arbitrary"),
        name="ffn_in_step",
    )(*args)


def _kvf_kernel(x_ref, nw_ref, wk_ref, wv_ref, wf_ref, bf_ref, tri_ref,
                k_ref, v_ref, kb_ref, vb_ref, lf_ref, fc_ref, xn_scr, carry_scr):
    li = pl.program_id(1)

    @pl.when(li == 0)
    def _():
        carry_scr[...] = jnp.zeros_like(carry_scr)

    xn_scr[...] = _rms(x_ref[0], nw_ref[...]).astype(BF16)
    xn = xn_scr[...]
    for c0, cw in _col_chunks(wk_ref.shape[1]):
        k = _dot(xn, wk_ref[:, c0:c0 + cw])
        k_ref[0, :, c0:c0 + cw] = k
        kb_ref[0, :, c0:c0 + cw] = k.astype(BF16)
        v = _dot(xn, wv_ref[:, c0:c0 + cw])
        v_ref[0, :, c0:c0 + cw] = v
        vb_ref[0, :, c0:c0 + cw] = v.astype(BF16)
    f = _dot(xn, wf_ref[...]) + bf_ref[...]
    lf = -_softplus(-f)
    lf_ref[0] = lf
    fc = _dot3_sel_x(tri_ref[...], lf) + carry_scr[0:1, :]
    fc_ref[0] = fc
    tm = fc.shape[0]
    carry_scr[...] = jnp.broadcast_to(fc[tm - 1:tm, :], carry_scr.shape)


def _kvf(h, nw, wk, wv, wf, bf, tri, tm):
    bsz, seq, d = h.shape
    nk = wk.shape[1]
    assert seq % tm == 0
    row = lambda n: pl.BlockSpec((1, tm, n), lambda b, l: (b, l, 0))
    ins = (h, nw, wk, wv, wf, bf, tri)
    return pl.pallas_call(
        _kvf_kernel,
        grid=(bsz, seq // tm),
        in_specs=[row(d)] + [_const_spec(a.shape) for a in ins[1:]],
        out_specs=[row(nk), row(nk), row(nk), row(nk), row(LANES), row(LANES)],
        out_shape=[jax.ShapeDtypeStruct((bsz, seq, nk), F32), jax.ShapeDtypeStruct((bsz, seq, nk), F32),
                   jax.ShapeDtypeStruct((bsz, seq, nk), BF16), jax.ShapeDtypeStruct((bsz, seq, nk), BF16),
                   jax.ShapeDtypeStruct((bsz, seq, LANES), F32), jax.ShapeDtypeStruct((bsz, seq, LANES), F32)],
        scratch_shapes=[pltpu.VMEM((tm, d), BF16), pltpu.VMEM((SUBLANES, LANES), F32)],
        compiler_params=_cparams("arbitrary", "arbitrary"),
        name="kvf_proj",
    )(*ins)


def _q_proj_kernel(scale, x_ref, nw_ref, w_ref, q_ref):
    xn = _rms(x_ref[...], nw_ref[...]).astype(BF16)
    for c0, cw in _col_chunks(w_ref.shape[1]):
        q_ref[:, c0:c0 + cw] = (_dot(xn, w_ref[:, c0:c0 + cw]) * scale).astype(q_ref.dtype)


def _q_proj(h, nw, w, scale, tm, out_dtype):
    m, d = h.shape
    n = w.shape[1]
    assert m % tm == 0
    row = lambda c: pl.BlockSpec((tm, c), lambda i: (i, 0))
    return pl.pallas_call(
        functools.partial(_q_proj_kernel, scale),
        grid=(m // tm,),
        in_specs=[row(d), _const_spec(nw.shape), _const_spec(w.shape)],
        out_specs=row(n),
        out_shape=jax.ShapeDtypeStruct((m, n), out_dtype),
        compiler_params=_cparams("arbitrary"),
        name="q_proj",
    )(h, nw, w)


LOG2E = 1.4426950408889634
N_SPLIT = 3
KV_BUILD_ROWS = 512


def _bias_selectors(hp, key_side):
    hrow = lax.broadcasted_iota(jnp.int32, (LANES, LANES), 0)
    lcol = lax.broadcasted_iota(jnp.int32, (LANES, LANES), 1)
    off = N_SPLIT if key_side else 0
    sels = []
    for i in range(2):
        per_head = []
        for r in range(N_SPLIT):
            hit = jnp.where(hrow == 2 * hp + i, lcol, -1) == 2 * N_SPLIT * i + off + r
            per_head.append(jnp.where(hit, 1.0, 0.0).astype(BF16))
        sels.append(per_head)
    lane = lax.broadcasted_iota(jnp.int32, (1, LANES), 1)
    ones = []
    for i in range(2):
        lo = 2 * N_SPLIT * i + (0 if key_side else N_SPLIT)
        ones.append(jnp.where((lane >= lo) & (lane < lo + N_SPLIT), 1.0, 0.0))
    return sels, ones


def _fox_prompt_kernel(q_ref, k_ref, v_ref, fc_ref, o_ref, kx_scr, qx_scr, sa_scr, sb_scr, m0_scr, m1_scr, l_scr,
                       acc_scr):
    hp = pl.program_id(1)
    qi = pl.program_id(2)
    tq = q_ref.shape[1]
    tk = tq
    seq = k_ref.shape[1]
    lane = lax.broadcasted_iota(jnp.int32, (tq, LANES), 1)
    first = lane < ATT_HEAD_DIM

    @pl.when(qi == 0)
    def _():
        sels, ones = _bias_selectors(hp, True)
        both = [sels[0][r] + sels[1][r] for r in range(N_SPLIT)]
        one_row = ones[0] + ones[1]

        def build(c, carry):
            r0 = pl.multiple_of(c * KV_BUILD_ROWS, KV_BUILD_ROWS)
            parts = _split3(-(fc_ref[0, pl.ds(r0, KV_BUILD_ROWS), :] * LOG2E))
            ext = one_row + _dot(parts[0], both[0])
            for r in range(1, N_SPLIT):
                ext = ext + _dot(parts[r], both[r])
            kx_scr[pl.ds(r0, KV_BUILD_ROWS), 0:LANES] = k_ref[0, pl.ds(r0, KV_BUILD_ROWS), :]
            kx_scr[pl.ds(r0, KV_BUILD_ROWS), LANES:2 * LANES] = ext.astype(BF16)
            return carry

        lax.fori_loop(0, seq // KV_BUILD_ROWS, build, 0)

    q = q_ref[0]
    zq = jnp.zeros_like(q)
    sels, ones = _bias_selectors(hp, False)
    q0 = pl.multiple_of(qi * tq, tq)
    fparts = _split3(fc_ref[0, pl.ds(q0, tq), :] * LOG2E)
    for i in range(2):
        ext = ones[i] + _dot(fparts[0], sels[i][0])
        for r in range(1, N_SPLIT):
            ext = ext + _dot(fparts[r], sels[i][r])
        qh = jnp.where(first, q, zq) if i == 0 else jnp.where(first, zq, q)
        qx_scr[i] = jnp.concatenate([qh, ext.astype(BF16)], axis=1)

    m0_scr[...] = jnp.full_like(m0_scr, -jnp.inf)
    m1_scr[...] = jnp.full_like(m1_scr, -jnp.inf)
    l_scr[...] = jnp.zeros_like(l_scr)
    acc_scr[...] = jnp.zeros_like(acc_scr)

    first_k = lax.broadcasted_iota(jnp.int32, (tk, LANES), 1) < ATT_HEAD_DIM
    ones_top = jnp.where(first_k, 1.0, 0.0).astype(BF16)
    ones_bot = jnp.where(first_k, 0.0, 1.0).astype(BF16)
    m_scrs = (m0_scr, m1_scr)

    def qk_into(buf, kj):
        start = pl.multiple_of(kj * tk, tk)
        kx = kx_scr[pl.ds(start, tk), :]
        for i in range(2):
            buf[i * tq:(i + 1) * tq, :] = _dot_nt(qx_scr[i], kx)

    def consume(buf, kj, masked):
        start = pl.multiple_of(kj * tk, tk)
        v = v_ref[0, pl.ds(start, tk), :]
        zv = jnp.zeros_like(v)
        vx = jnp.concatenate([jnp.concatenate([jnp.where(first_k, v, zv), ones_top], axis=1),
                              jnp.concatenate([jnp.where(first_k, zv, v), ones_bot], axis=1)], axis=0)
        ps, alphas = [], []
        for i in range(2):
            chunks = [buf[i * tq:(i + 1) * tq, c0:c0 + LANES] for c0 in range(0, tk, LANES)]
            if masked:
                r = lax.broadcasted_iota(jnp.int32, (tq, LANES), 0)
                c = lax.broadcasted_iota(jnp.int32, (tq, LANES), 1)
                chunks = [jnp.where(r >= c + c0, ch, -jnp.inf) for ch, c0 in zip(chunks, range(0, tk, LANES))]
            m_old = m_scrs[i][...]
            m_new = jnp.maximum(m_old, jnp.max(functools.reduce(jnp.maximum, chunks), axis=1, keepdims=True))
            alphas.append(jnp.exp2(m_old - m_new))
            ps.append(jnp.concatenate([jnp.exp2(ch - m_new).astype(BF16) for ch in chunks], axis=1))
            m_scrs[i][...] = m_new
        tot = _dot(jnp.concatenate(ps, axis=1), vx)
        alpha = jnp.where(first, alphas[0], alphas[1])
        acc_scr[...] = alpha * acc_scr[...] + tot[:, :LANES]
        l_scr[...] = alpha * l_scr[...] + tot[:, LANES:]

    qk_into(sa_scr, 0)
    n_pairs = qi // 2

    def body(t, c):
        qk_into(sb_scr, 2 * t + 1)
        consume(sa_scr, 2 * t, False)
        qk_into(sa_scr, 2 * t + 2)
        consume(sb_scr, 2 * t + 1, False)
        return c

    lax.fori_loop(0, n_pairs, body, 0)

    @pl.when(qi % 2 == 0)
    def _():
        consume(sa_scr, qi, True)

    @pl.when(qi % 2 == 1)
    def _():
        qk_into(sb_scr, qi)
        consume(sa_scr, qi - 1, False)
        consume(sb_scr, qi, True)

    o_ref[0] = (acc_scr[...] / l_scr[...]).astype(o_ref.dtype)


def _fox_prompt(q, kb, vb, fc, tq):
    bsz, seq, dm = q.shape
    n_pairs = dm // LANES
    assert seq % tq == 0 and seq % KV_BUILD_ROWS == 0
    nq = seq // tq
    return pl.pallas_call(
        _fox_prompt_kernel,
        grid=(bsz, n_pairs, nq),
        in_specs=[pl.BlockSpec((1, tq, LANES), lambda b, p, i: (b, i, p)),
                  pl.BlockSpec((1, seq, LANES), lambda b, p, i: (b, 0, p)),
                  pl.BlockSpec((1, seq, LANES), lambda b, p, i: (b, 0, p)),
                  pl.BlockSpec((1, seq, LANES), lambda b, p, i: (b, 0, 0))],
        out_specs=pl.BlockSpec((1, tq, LANES), lambda b, p, i: (b, i, p)),
        out_shape=jax.ShapeDtypeStruct((bsz, seq, dm), BF16),
        scratch_shapes=[pltpu.VMEM((seq, 2 * LANES), BF16), pltpu.VMEM((2, tq, 2 * LANES), BF16),
                        pltpu.VMEM((2 * tq, tq), F32), pltpu.VMEM((2 * tq, tq), F32),
                        pltpu.VMEM((tq, LANES), F32), pltpu.VMEM((tq, LANES), F32),
                        pltpu.VMEM((tq, LANES), F32), pltpu.VMEM((tq, LANES), F32)],
        compiler_params=_cparams("arbitrary", "arbitrary", "arbitrary"),
        name="fox_prompt",
    )(q, kb, vb, fc)


def _fox_sample_kernel(n_group, n_new, pt_ref, q_ref, kn_ref, vn_ref, lfn_ref, suf_ref, *refs):
    k_refs = refs[0:n_group]
    v_refs = refs[n_group:2 * n_group]
    lf_refs = refs[2 * n_group:3 * n_group]
    o_ref, m_scr, l_scr, acc_scr, fn_scr, carry_scr = refs[3 * n_group:]
    i = pl.program_id(1)
    n_heads, n_rows, dh = q_ref.shape[1:]
    page = suf_ref.shape[0]

    @pl.when(i == 0)
    def _():
        y = lfn_ref[0]
        t3 = lax.broadcasted_iota(jnp.int32, y.shape, 1)
        sh = 1
        while sh < n_new:
            y = y + jnp.where(t3 >= sh, pltpu.roll(y, sh, 1), 0.0)
            sh *= 2
        fn_scr[...] = y
        q = q_ref[0]
        t1 = lax.broadcasted_iota(jnp.int32, (n_heads, n_rows, 1), 1)
        fcol = y[:, :, 0:1]
        ss = []
        for u in range(n_new):
            s_u = jnp.sum(q * kn_ref[0, :, u:u + 1, :], axis=2, keepdims=True) + fcol - fcol[:, u:u + 1, :]
            ss.append(jnp.where(t1 >= u, s_u, -jnp.inf))
        m0 = functools.reduce(jnp.maximum, ss)
        l0 = jnp.zeros_like(m0)
        acc0 = jnp.zeros((n_heads, n_rows, dh), F32)
        for u in range(n_new):
            p_u = jnp.exp(ss[u] - m0)
            l0 = l0 + p_u
            acc0 = acc0 + p_u * vn_ref[0, :, u:u + 1, :]
        m_scr[...] = jnp.broadcast_to(m0, m_scr.shape)
        l_scr[...] = jnp.broadcast_to(l0, l_scr.shape)
        acc_scr[...] = acc0
        carry_scr[...] = jnp.zeros_like(carry_scr)

    suf = suf_ref[...]
    withins, prefixes = [], []
    run = carry_scr[:, 0:1]
    for g in range(n_group):
        lf_t = lf_refs[g][0]
        withins.append(_dot3_x_sel(lf_t, suf))
        prefixes.append(run)
        run = run + jnp.sum(lf_t, axis=1, keepdims=True)
    carry_scr[...] = jnp.broadcast_to(run, carry_scr.shape)

    shift = jnp.concatenate([withins[g] + prefixes[g] for g in range(n_group)], axis=1)
    fn = jnp.concatenate([fn_scr[...]] * n_group, axis=2)
    s3 = jnp.stack([
        _dot(q_ref[0, h].astype(BF16),
             jnp.concatenate([k_refs[g][0, h] for g in range(n_group)], axis=1).astype(BF16)) + shift[h:h + 1, :]
        for h in range(n_heads)]) + fn
    m_old = m_scr[...]
    m_new = jnp.maximum(m_old, jnp.max(s3, axis=2, keepdims=True))
    alpha = jnp.exp(m_old - m_new)
    p3 = jnp.exp(s3 - jnp.concatenate([m_new] * n_group, axis=2))
    l_scr[...] = alpha * l_scr[...] + jnp.sum(p3, axis=2, keepdims=True)
    m_scr[...] = m_new
    pb = p3.astype(BF16)
    pv = jnp.stack([
        _dot_nt(pb[h], jnp.concatenate([v_refs[g][0, h] for g in range(n_group)], axis=1).astype(BF16))
        for h in range(n_heads)])
    acc_scr[...] = alpha[:, :, 0:dh] * acc_scr[...] + pv

    @pl.when(i == pl.num_programs(1) - 1)
    def _():
        o_ref[0] = acc_scr[...] / l_scr[:, :, 0:dh]


def _fox_sample(q4, kn4, vn4, lfn4, cache_k, cache_v, cache_lf_t, page_table, suf, n_group, n_new):
    db, n_heads, n_rows, dh = q4.shape
    n_pages = page_table.shape[1]
    page = cache_lf_t.shape[2]
    assert n_pages % n_group == 0 and page == LANES and n_new <= n_rows
    pt_flat = page_table.reshape(-1)

    def page_map(g, ndim):
        def imap(b, i, pt):
            return (pt[b * n_pages + (n_pages - 1 - (i * n_group + g))],) + (0,) * (ndim - 1)
        return imap

    seq_spec = lambda c: pl.BlockSpec((1, n_heads, n_rows, c), lambda b, i, pt: (b, 0, 0, 0))
    in_specs = [seq_spec(dh), seq_spec(dh), seq_spec(dh), seq_spec(LANES),
                pl.BlockSpec(suf.shape, lambda b, i, pt: (0, 0))]
    in_specs += [pl.BlockSpec((1, n_heads, dh, page), page_map(g, 4)) for g in range(n_group)]
    in_specs += [pl.BlockSpec((1, n_heads, dh, page), page_map(g, 4)) for g in range(n_group)]
    in_specs += [pl.BlockSpec((1, n_heads, page), page_map(g, 3)) for g in range(n_group)]
    grid_spec = pltpu.PrefetchScalarGridSpec(
        num_scalar_prefetch=1,
        grid=(db, n_pages // n_group),
        in_specs=in_specs,
        out_specs=seq_spec(dh),
        scratch_shapes=[pltpu.VMEM((n_heads, n_rows, LANES), F32), pltpu.VMEM((n_heads, n_rows, LANES), F32),
                        pltpu.VMEM((n_heads, n_rows, dh), F32), pltpu.VMEM((n_heads, n_rows, LANES), F32),
                        pltpu.VMEM((n_heads, LANES), F32)],
    )
    return pl.pallas_call(
        functools.partial(_fox_sample_kernel, n_group, n_new),
        grid_spec=grid_spec,
        out_shape=jax.ShapeDtypeStruct((db, n_heads, n_rows, dh), F32),
        compiler_params=_cparams("arbitrary", "arbitrary"),
        name="fox_sample",
    )(pt_flat, q4, kn4, vn4, lfn4, suf, *([cache_k] * n_group), *([cache_v] * n_group),
      *([cache_lf_t] * n_group))


def _prep_weights(norm_w, m_w_in, m_conv_w, m_conv_b, m_dt_bias, m_a_log, m_d, m_gnorm_w, m_w_out,
                  ffn_w_in, ffn_conv_w, ffn_conv_b, ffn_w_out, kv_norm_w, w_kvf, b_f, w_q, w_o):
    d_model = norm_w.shape[-1]
    d_inner = m_w_out.shape[1]
    n_sheads = m_dt_bias.shape[1]
    conv_dim = m_conv_w.shape[2]
    d_ff = ffn_w_out.shape[1]
    att_dim = w_q.shape[2]
    n_aheads = b_f.shape[0]
    pad_lanes = lambda a: jnp.pad(a, [(0, 0)] * (a.ndim - 1) + [(0, LANES - a.shape[-1])])
    w = {}
    w["norm"] = norm_w.reshape(norm_w.shape[0], norm_w.shape[1], 1, d_model)
    w["m_wz"] = m_w_in[:, :, :d_inner].astype(BF16)
    w["m_wx"] = m_w_in[:, :, d_inner:d_inner + conv_dim].astype(BF16)
    w["m_wdt"] = pad_lanes(m_w_in[:, :, d_inner + conv_dim:]).astype(BF16)
    w["m_conv_w"] = m_conv_w
    w["m_conv_b"] = m_conv_b[:, None, :]
    w["m_dtb"] = pad_lanes(m_dt_bias)[:, None, :]
    w["m_alog"] = pad_lanes(m_a_log)[:, None, :]
    w["m_dsk"] = jnp.repeat(m_d.astype(F32), SSM_HEAD_DIM, axis=1)[:, None, :]
    w["m_gw"] = m_gnorm_w[:, None, :]
    w["m_wout"] = m_w_out.astype(BF16)
    w["f_wg"] = ffn_w_in[:, :, :d_ff].astype(BF16)
    w["f_wu"] = ffn_w_in[:, :, d_ff:].astype(BF16)
    w["f_conv_w"] = ffn_conv_w
    w["f_conv_b"] = ffn_conv_b[:, None, :]
    w["f_wout"] = ffn_w_out.astype(BF16)
    w["kv_norm"] = kv_norm_w[None, :]
    w["wk"] = w_kvf[:, :att_dim].astype(BF16)
    w["wv"] = w_kvf[:, att_dim:2 * att_dim].astype(BF16)
    w["wf"] = pad_lanes(w_kvf[:, 2 * att_dim:]).astype(BF16)
    w["bf"] = pad_lanes(b_f[None, :])
    w["wq"] = w_q.astype(BF16)
    w["wo"] = w_o.astype(BF16)
    hh = jnp.arange(LANES)[:, None]
    cc = jnp.arange(d_inner)[None, :] // SSM_HEAD_DIM
    w["ex"] = (hh == cc).astype(BF16)
    r = jnp.arange(SSD_CHUNK)
    w["tri_chunk"] = (r[:, None] >= r[None, :]).astype(BF16)
    p = jnp.arange(PAGE_SIZE)
    w["suf"] = (p[:, None] > p[None, :]).astype(BF16)
    w["n_sheads"] = n_sheads
    w["n_aheads"] = n_aheads
    return w


TM_SEQ = 1024
TM_MAMBA_IN = 512
SSD_SUB = 4
TQ = 512
N_PAGE_GROUP = 8


def _tri(n):
    r = jnp.arange(n)
    return (r[:, None] >= r[None, :]).astype(BF16)


def _prompt_trunk(x, w, depth, n_a):
    bsz, seq, d = x.shape
    m = bsz * seq
    h = x
    d_inner = w["m_wout"].shape[1]
    outs_ssm, outs_mconv, outs_ffn = [], [], []
    kf = vf = kb = vb = lf = fc = None
    for i in range(depth):
        nw = w["norm"][i]
        if i < n_a:
            z, xa, dt, cst = _mamba_in_seq(h, nw[0], w["m_wz"][i], w["m_wx"][i], w["m_wdt"][i],
                                           w["m_conv_w"][i], w["m_conv_b"][i], w["m_dtb"][i], TM_MAMBA_IN)
            s0 = jnp.zeros((bsz, d_inner, D_STATE), F32)
            y, s_new = _ssd(xa, z, dt, s0, w["m_alog"][i], w["m_dsk"][i], w["ex"], w["tri_chunk"], SSD_SUB)
            h = _out_proj(y.reshape(m, d_inner), w["m_wout"][i], h.reshape(m, d), nw[1], TM_SEQ,
                          gw=w["m_gw"][i]).reshape(bsz, seq, d)
            width = w["m_conv_w"].shape[1]
            outs_mconv.append(cst[:, SUBLANES - (width - 1):, :])
            outs_ssm.append(s_new.reshape(bsz, w["n_sheads"], SSM_HEAD_DIM, D_STATE))
        else:
            j = i - n_a
            q = _q_proj(h.reshape(m, d), nw[0], w["wq"][j], ATT_HEAD_DIM ** -0.5 * LOG2E, TM_SEQ, BF16)
            o = _fox_prompt(q.reshape(bsz, seq, -1), kb, vb, fc, TQ)
            h = _out_proj(o.reshape(m, -1), w["wo"][j], h.reshape(m, d), nw[1], TM_SEQ).reshape(bsz, seq, d)
        act, fst = _ffn_in_seq(h, nw[2], w["f_wg"][i], w["f_wu"][i], w["f_conv_w"][i], w["f_conv_b"][i], TM_SEQ)
        h = _out_proj(act.reshape(m, -1), w["f_wout"][i], h.reshape(m, d), nw[3], TM_SEQ).reshape(bsz, seq, d)
        fwidth = w["f_conv_w"].shape[1]
        outs_ffn.append(fst[:, SUBLANES - (fwidth - 1):, :])
        if i == n_a - 1:
            kf, vf, kb, vb, lf, fc = _kvf(h, w["kv_norm"], w["wk"], w["wv"], w["wf"], w["bf"], _tri(TM_SEQ), TM_SEQ)
    n_ah = w["n_aheads"]
    return (h, jnp.stack(outs_ssm), jnp.stack(outs_mconv), jnp.stack(outs_ffn),
            kf.reshape(bsz, seq, n_ah, ATT_HEAD_DIM), vf.reshape(bsz, seq, n_ah, ATT_HEAD_DIM), lf[:, :, :n_ah])


def _to_tm(a):
    db, n_new, c = a.shape
    return jnp.swapaxes(a, 0, 1).reshape(n_new * db, c)


def _from_tm(a, db):
    m, c = a.shape
    return jnp.swapaxes(a.reshape(m // db, db, c), 0, 1)


def _heads_major(a, n_heads):
    db, n_new, c = a.shape
    a4 = jnp.swapaxes(a.reshape(db, n_new, n_heads, c // n_heads), 1, 2)
    return jnp.pad(a4, ((0, 0), (0, 0), (0, SUBLANES - n_new), (0, 0)))


def _sample_trunk(x, ssm_st, mconv_st, ffn_st, cache_k, cache_v, cache_logf, page_table, w, depth, n_a):
    db, n_new, d = x.shape
    m = db * n_new
    h = _to_tm(x)
    d_inner = w["m_wout"].shape[1]
    n_ah = w["n_aheads"]
    outs_ssm, outs_mconv, outs_ffn = [], [], []
    kf = vf = lf = None
    for i in range(depth):
        nw = w["norm"][i]
        if i < n_a:
            st_tm = jnp.swapaxes(mconv_st[i], 0, 1)
            z, xa, dt, nst = _mamba_in_step(h, st_tm, nw[0], w["m_wz"][i], w["m_wx"][i], w["m_wdt"][i],
                                            w["m_conv_w"][i], w["m_conv_b"][i], w["m_dtb"][i], n_new)
            s0 = ssm_st[i].reshape(db, d_inner, D_STATE)
            y, s_new = _ssd(_from_tm(xa, db), _from_tm(z, db), _from_tm(dt, db), s0,
                            w["m_alog"][i], w["m_dsk"][i], w["ex"], w["tri_chunk"], 1)
            y_tm = _to_tm(y)
            h = _out_proj(y_tm, w["m_wout"][i], h, nw[1], m, gw=w["m_gw"][i])
            outs_mconv.append(jnp.swapaxes(nst, 0, 1))
            outs_ssm.append(s_new.reshape(db, w["n_sheads"], SSM_HEAD_DIM, D_STATE))
        else:
            j = i - n_a
            q = _from_tm(_q_proj(h, nw[0], w["wq"][j], ATT_HEAD_DIM ** -0.5, m, F32), db)
            ck_t = jnp.transpose(cache_k, (0, 2, 3, 1))
            cv_t = jnp.transpose(cache_v, (0, 2, 3, 1))
            clf_t = jnp.swapaxes(cache_logf, 1, 2)
            lfn3 = jnp.pad(jnp.swapaxes(lf[:, :, :n_ah], 1, 2), ((0, 0), (0, 0), (0, SUBLANES - n_new)))
            lfn4 = jnp.broadcast_to(lfn3[..., None], (db, n_ah, SUBLANES, LANES))
            o4 = _fox_sample(_heads_major(q, n_ah), _heads_major(kf, n_ah), _heads_major(vf, n_ah), lfn4,
                             ck_t, cv_t, clf_t, page_table, w["suf"], N_PAGE_GROUP, n_new)
            o = jnp.swapaxes(o4[:, :, :n_new, :], 1, 2).reshape(db, n_new, -1)
            h = _out_proj(_to_tm(o), w["wo"][j], h, nw[1], m)
        fst_tm = jnp.swapaxes(ffn_st[i], 0, 1)
        act, nfst = _ffn_in_step(h, fst_tm, nw[2], w["f_wg"][i], w["f_wu"][i], w["f_conv_w"][i],
                                 w["f_conv_b"][i], n_new)
        h = _out_proj(act, w["f_wout"][i], h, nw[3], m)
        outs_ffn.append(jnp.swapaxes(nfst, 0, 1))
        if i == n_a - 1:
            kf_tm, vf_tm, _, _, lf_tm, _ = _kvf(h[None], w["kv_norm"], w["wk"], w["wv"], w["wf"], w["bf"],
                                                _tri(m), m)
            kf, vf, lf = _from_tm(kf_tm[0], db), _from_tm(vf_tm[0], db), _from_tm(lf_tm[0], db)
    return (_from_tm(h, db), jnp.stack(outs_ssm), jnp.stack(outs_mconv), jnp.stack(outs_ffn),
            kf.reshape(db, n_new, n_ah, ATT_HEAD_DIM), vf.reshape(db, n_new, n_ah, ATT_HEAD_DIM), lf[:, :, :n_ah])


def kernel(x_prompt, x_sample, state_ssm, state_mconv, state_ffn, cache_k, cache_v, cache_logf, page_table,
           norm_w, m_w_in, m_conv_w, m_conv_b, m_dt_bias, m_a_log, m_d, m_gnorm_w, m_w_out,
           ffn_w_in, ffn_conv_w, ffn_conv_b, ffn_w_out, kv_norm_w, w_kvf, b_f, w_q, w_o):
    depth = norm_w.shape[0]
    n_a = m_w_in.shape[0]
    w = _prep_weights(norm_w, m_w_in, m_conv_w, m_conv_b, m_dt_bias, m_a_log, m_d, m_gnorm_w, m_w_out,
                      ffn_w_in, ffn_conv_w, ffn_conv_b, ffn_w_out, kv_norm_w, w_kvf, b_f, w_q, w_o)
    y_p, p_ssm, p_mconv, p_ffn, p_k, p_v, p_lf = _prompt_trunk(x_prompt, w, depth, n_a)
    y_s, s_ssm, s_mconv, s_ffn, s_k, s_v, s_lf = _sample_trunk(
        x_sample, state_ssm, state_mconv, state_ffn, cache_k, cache_v, cache_logf, page_table, w, depth, n_a)
    return (y_p, y_s, p_ssm, p_mconv, p_ffn, p_k, p_v, p_lf, s_ssm, s_mconv, s_ffn, s_k, s_v, s_lf)
```

```python
import functools
import math

import jax
import jax.numpy as jnp
from jax import lax
from jax.experimental import pallas as pl
from jax.experimental.pallas import tpu as pltpu

F32 = jnp.float32
BF16 = jnp.bfloat16
EPS = 1e-6

LANES = 128
SUBLANES = 8
VMEM_LIMIT_BYTES = 56 * 1024 * 1024

SSM_HEAD_DIM = 64
SSM_GROUPS = 4
D_STATE = 128
SSD_CHUNK = 128
ATT_HEAD_DIM = 64
PAGE_SIZE = 128


def _cparams(*sem):
    return pltpu.CompilerParams(dimension_semantics=sem, vmem_limit_bytes=VMEM_LIMIT_BYTES)


def _const_spec(shape):
    nd = len(shape)
    return pl.BlockSpec(shape, lambda *_: (0,) * nd)


def _rms(x, w):
    ms = jnp.mean(x * x, axis=-1, keepdims=True)
    return x * lax.rsqrt(ms + EPS) * w


def _silu(x):
    return x / (1.0 + jnp.exp(-x))


def _softplus(x):
    return jnp.maximum(x, 0.0) + jnp.log1p(jnp.exp(-jnp.abs(x)))


def _dot(a, b):
    return jnp.dot(a, b, preferred_element_type=F32)


def _dot_nt(a, b):
    return lax.dot_general(a, b, (((1,), (1,)), ((), ())), preferred_element_type=F32)


def _dot_tn(a, b):
    return lax.dot_general(a, b, (((0,), (0,)), ((), ())), preferred_element_type=F32)


def _split3(x):
    hi = x.astype(BF16)
    r1 = x - hi.astype(F32)
    mid = r1.astype(BF16)
    lo = (r1 - mid.astype(F32)).astype(BF16)
    return hi, mid, lo


def _dot3_x_sel(x, sel):
    hi, mid, lo = _split3(x)
    return _dot(hi, sel) + _dot(mid, sel) + _dot(lo, sel)


def _dot3_sel_x(sel, x):
    hi, mid, lo = _split3(x)
    return _dot(sel, hi) + _dot(sel, mid) + _dot(sel, lo)


def _conv_rows(g, carry8, w, b, gbuf):
    tm, n = g.shape
    width = w.shape[0]
    gbuf[0:SUBLANES, 0:n] = carry8
    gbuf[SUBLANES:SUBLANES + tm, 0:n] = g
    out = b + g * w[width - 1:width]
    for k in range(width - 1):
        start = SUBLANES - (width - 1 - k)
        out = out + gbuf[start:start + tm, 0:n] * w[k:k + 1]
    return out


def _conv_slabs(slabs, w, b, n_new):
    width = w.shape[0]
    outs = []
    for t in range(n_new):
        o = b + slabs[t] * w[0:1]
        for k in range(1, width):
            o = o + slabs[t + k] * w[k:k + 1]
        outs.append(o)
    return outs


COL_CHUNK = 512


def _col_chunks(n, step=COL_CHUNK):
    out, c = [], 0
    while c < n:
        w = min(step, n - c)
        out.append((c, w))
        c += w
    return out


def _mamba_in_seq_kernel(x_ref, nw_ref, wz_ref, wx_ref, wdt_ref, cw_ref, cb_ref, dtb_ref,
                         z_ref, xa_ref, dt_ref, st_ref, xn_scr, carry_scr, gbuf_scr):
    li = pl.program_id(1)

    @pl.when(li == 0)
    def _():
        carry_scr[...] = jnp.zeros_like(carry_scr)

    xn_scr[...] = _rms(x_ref[0], nw_ref[...]).astype(BF16)
    xn = xn_scr[...]
    tm = xn.shape[0]
    for c0, cw in _col_chunks(wz_ref.shape[1]):
        z_ref[0, :, c0:c0 + cw] = _dot(xn, wz_ref[:, c0:c0 + cw])
    for c0, cw in _col_chunks(wx_ref.shape[1]):
        g = _dot(xn, wx_ref[:, c0:c0 + cw])
        carry8 = carry_scr[:, c0:c0 + cw]
        y = _conv_rows(g, carry8, cw_ref[:, c0:c0 + cw], cb_ref[:, c0:c0 + cw], gbuf_scr)
        xa_ref[0, :, c0:c0 + cw] = _silu(y)
        carry_scr[:, c0:c0 + cw] = g[tm - SUBLANES:]
    dt_ref[0] = _softplus(_dot(xn, wdt_ref[...]) + dtb_ref[...])
    st_ref[0] = carry_scr[...]


def _mamba_in_seq(h, nw, wz, wx, wdt, conv_w, conv_b, dtb, tm):
    bsz, seq, d = h.shape
    nz, nx = wz.shape[1], wx.shape[1]
    assert seq % tm == 0
    grid = (bsz, seq // tm)
    row = lambda n: pl.BlockSpec((1, tm, n), lambda b, l: (b, l, 0))
    return pl.pallas_call(
        _mamba_in_seq_kernel,
        grid=grid,
        in_specs=[row(d), _const_spec(nw.shape), _const_spec(wz.shape), _const_spec(wx.shape),
                  _const_spec(wdt.shape), _const_spec(conv_w.shape), _const_spec(conv_b.shape),
                  _const_spec(dtb.shape)],
        out_specs=[row(nz), row(nx), row(LANES),
                   pl.BlockSpec((1, SUBLANES, nx), lambda b, l: (b, 0, 0))],
        out_shape=[jax.ShapeDtypeStruct((bsz, seq, nz), F32),
                   jax.ShapeDtypeStruct((bsz, seq, nx), F32),
                   jax.ShapeDtypeStruct((bsz, seq, LANES), F32),
                   jax.ShapeDtypeStruct((bsz, SUBLANES, nx), F32)],
        scratch_shapes=[pltpu.VMEM((tm, d), BF16), pltpu.VMEM((SUBLANES, nx), F32),
                        pltpu.VMEM((SUBLANES + tm, COL_CHUNK), F32)],
        compiler_params=_cparams("arbitrary", "arbitrary"),
        name="mamba_in_seq",
    )(h, nw, wz, wx, wdt, conv_w, conv_b, dtb)


def _mamba_in_step_kernel(n_new, x_ref, st_ref, nw_ref, wz_ref, wx_ref, wdt_ref, cw_ref, cb_ref, dtb_ref,
                          z_ref, xa_ref, dt_ref, nst_ref):
    xn = _rms(x_ref[...], nw_ref[...]).astype(BF16)
    nb = xn.shape[0] // n_new
    width = cw_ref.shape[0]
    for c0, cw in _col_chunks(wz_ref.shape[1]):
        z_ref[:, c0:c0 + cw] = _dot(xn, wz_ref[:, c0:c0 + cw])
    for c0, cw in _col_chunks(wx_ref.shape[1]):
        g = _dot(xn, wx_ref[:, c0:c0 + cw])
        slabs = [st_ref[k, :, c0:c0 + cw] for k in range(width - 1)]
        slabs += [g[t * nb:(t + 1) * nb] for t in range(n_new)]
        outs = _conv_slabs(slabs, cw_ref[:, c0:c0 + cw], cb_ref[:, c0:c0 + cw], n_new)
        for t in range(n_new):
            xa_ref[t * nb:(t + 1) * nb, c0:c0 + cw] = _silu(outs[t])
        for k in range(width - 1):
            nst_ref[k, :, c0:c0 + cw] = slabs[n_new + k]
    dt_ref[...] = _softplus(_dot(xn, wdt_ref[...]) + dtb_ref[...])


def _mamba_in_step(x_tm, st_tm, nw, wz, wx, wdt, conv_w, conv_b, dtb, n_new):
    m, d = x_tm.shape
    nz, nx = wz.shape[1], wx.shape[1]
    args = (x_tm, st_tm, nw, wz, wx, wdt, conv_w, conv_b, dtb)
    return pl.pallas_call(
        functools.partial(_mamba_in_step_kernel, n_new),
        grid=(1,),
        in_specs=[_const_spec(a.shape) for a in args],
        out_specs=[_const_spec((m, nz)), _const_spec((m, nx)), _const_spec((m, LANES)),
                   _const_spec(st_tm.shape)],
        out_shape=[jax.ShapeDtypeStruct((m, nz), F32), jax.ShapeDtypeStruct((m, nx), F32),
                   jax.ShapeDtypeStruct((m, LANES), F32), jax.ShapeDtypeStruct(st_tm.shape, F32)],
        compiler_params=_cparams("arbitrary"),
        name="mamba_in_step",
    )(*args)


def _ssd_kernel(valid, n_sub, xa_ref, z_ref, dt_ref, s0_ref, alog_ref, dsk_ref, ex_ref, tri_ref,
                y_ref, sout_ref, st_scr, *pad_scrs):
    ci = pl.program_id(1)
    d_inner = z_ref.shape[2]
    if pad_scrs:
        t_rows = pad_scrs[0].shape[0]

        @pl.when((pl.program_id(0) == 0) & (ci == 0))
        def _():
            for p in pad_scrs:
                p[...] = jnp.zeros_like(p)

        for p, r in zip(pad_scrs, (xa_ref, z_ref, dt_ref)):
            p[0:valid, :] = r[0]
        xa_at = lambda rs, cs: pad_scrs[0][rs, cs]
        z_at = lambda rs, cs: pad_scrs[1][rs, cs]
        dt_at = lambda rs: pad_scrs[2][rs, :]
    else:
        t_rows = xa_ref.shape[1] // n_sub
        xa_at = lambda rs, cs: xa_ref[0, rs, cs]
        z_at = lambda rs, cs: z_ref[0, rs, cs]
        dt_at = lambda rs: dt_ref[0, rs, :]
    gn = SSM_GROUPS * D_STATE
    n_heads = d_inner // SSM_HEAD_DIM
    heads_per_group = n_heads // SSM_GROUPS
    gcols = heads_per_group * SSM_HEAD_DIM

    @pl.when(ci == 0)
    def _():
        for c0 in range(0, d_inner, LANES):
            st_scr[:, c0:c0 + LANES] = s0_ref[0, c0:c0 + LANES, :].T

    li = lax.broadcasted_iota(jnp.int32, (t_rows, t_rows), 0)
    si = lax.broadcasted_iota(jnp.int32, (t_rows, t_rows), 1)
    causal = li >= si
    lane = lax.broadcasted_iota(jnp.int32, (t_rows, LANES), 1)
    first_head = lane < SSM_HEAD_DIM
    ex = ex_ref[...]
    a_row = -jnp.exp(alog_ref[...])

    for sub in range(n_sub):
        rs = slice(sub * t_rows, (sub + 1) * t_rows)
        xs = xa_at(rs, slice(0, d_inner))
        bm = xa_at(rs, slice(d_inner, d_inner + gn)).astype(BF16)
        cm = xa_at(rs, slice(d_inner + gn, d_inner + 2 * gn)).astype(BF16)
        dt = dt_at(rs)
        if valid < t_rows:
            dt = jnp.where(lax.broadcasted_iota(jnp.int32, dt.shape, 0) < valid, dt, 0.0)
        dta = dt * a_row
        acum = _dot3_sel_x(tri_ref[...], dta)
        acum_t = acum.T
        dt_e = _dot3_x_sel(dt, ex)
        acum_e = _dot3_x_sel(acum, ex)
        xdt = xs * dt_e
        xdt_b = xdt.astype(BF16)
        last_e = acum_e[t_rows - 1:t_rows, :]
        e_acum = jnp.exp(acum_e)
        xde_b = (xdt * jnp.exp(last_e - acum_e)).astype(BF16)
        e_last = jnp.exp(last_e)

        for g in range(SSM_GROUPS):
            c_g = cm[:, g * D_STATE:(g + 1) * D_STATE]
            b_g = bm[:, g * D_STATE:(g + 1) * D_STATE]
            cb = _dot_nt(c_g, b_g)
            g0 = g * gcols
            pieces = []
            for k in range(heads_per_group // 2):
                h1 = g * heads_per_group + 2 * k
                ms = []
                for hh in (h1, h1 + 1):
                    seg = acum[:, hh:hh + 1] - acum_t[hh:hh + 1, :]
                    dec = jnp.exp(jnp.where(causal, seg, -jnp.inf))
                    ms.append((cb * dec).astype(BF16))
                x2 = xdt_b[:, g0 + k * LANES:g0 + (k + 1) * LANES]
                zero = jnp.zeros_like(x2)
                rhs = jnp.concatenate([jnp.where(first_head, x2, zero), jnp.where(first_head, zero, x2)], axis=0)
                pieces.append(_dot(jnp.concatenate(ms, axis=1), rhs))
            y_diag = jnp.concatenate(pieces, axis=1)
            st_g = st_scr[:, g0:g0 + gcols]
            y_off = _dot(c_g, st_g.astype(BF16)) * e_acum[:, g0:g0 + gcols]
            xs_g = xs[:, g0:g0 + gcols]
            y = y_diag + y_off + dsk_ref[:, g0:g0 + gcols] * xs_g
            gated = y * _silu(z_at(rs, slice(g0, g0 + gcols)))
            if pad_scrs:
                y_ref[0, :, g0:g0 + gcols] = gated[0:valid]
            else:
                y_ref[0, rs, g0:g0 + gcols] = gated
            st_scr[:, g0:g0 + gcols] = e_last[:, g0:g0 + gcols] * st_g + _dot_tn(b_g, xde_b[:, g0:g0 + gcols])

    @pl.when(ci == pl.num_programs(1) - 1)
    def _():
        for c0 in range(0, d_inner, LANES):
            sout_ref[0, c0:c0 + LANES, :] = st_scr[:, c0:c0 + LANES].T


def _ssd(xa, z, dt, s0, a_row, dsk_e, ex, tri, n_sub):
    bsz, seq, nx = xa.shape
    d_inner = z.shape[2]
    short = seq < SSD_CHUNK
    rows = seq if short else SSD_CHUNK * n_sub
    assert seq % rows == 0 and (n_sub == 1 or not short)
    row = lambda n: pl.BlockSpec((1, rows, n), lambda b, c: (b, c, 0))
    st_spec = pl.BlockSpec((1, d_inner, D_STATE), lambda b, c: (b, 0, 0))
    consts = (a_row, dsk_e, ex, tri)
    scratch = [pltpu.VMEM((D_STATE, d_inner), F32)]
    if short:
        scratch += [pltpu.VMEM((SSD_CHUNK, n), F32) for n in (nx, d_inner, LANES)]
    return pl.pallas_call(
        functools.partial(_ssd_kernel, seq if short else SSD_CHUNK, n_sub),
        grid=(bsz, seq // rows),
        in_specs=[row(nx), row(d_inner), row(LANES), st_spec] + [_const_spec(c.shape) for c in consts],
        out_specs=[row(d_inner), st_spec],
        out_shape=[jax.ShapeDtypeStruct((bsz, seq, d_inner), F32),
                   jax.ShapeDtypeStruct((bsz, d_inner, D_STATE), F32)],
        scratch_shapes=scratch,
        compiler_params=_cparams("arbitrary", "arbitrary"),
        name="ssd_chunk",
    )(xa, z, dt, s0, *consts)


def _out_proj_kernel(pre_norm, a_ref, *refs):
    if pre_norm:
        gw_ref, w_ref, h_ref, pw_ref, o_ref = refs
        a = _rms(a_ref[...].astype(F32), gw_ref[...]).astype(BF16)
    else:
        w_ref, h_ref, pw_ref, o_ref = refs
        a = a_ref[...].astype(BF16)
    o_ref[...] = h_ref[...] + _rms(_dot(a, w_ref[...]), pw_ref[...])


def _out_proj(a, w, h, pw, tm, gw=None):
    m, k = a.shape
    d = w.shape[1]
    assert m % tm == 0
    row = lambda n: pl.BlockSpec((tm, n), lambda i: (i, 0))
    ins = [a] + ([gw] if gw is not None else []) + [w, h, pw]
    specs = [row(k)] + ([_const_spec(gw.shape)] if gw is not None else []) + \
        [_const_spec(w.shape), row(d), _const_spec(pw.shape)]
    return pl.pallas_call(
        functools.partial(_out_proj_kernel, gw is not None),
        grid=(m // tm,),
        in_specs=specs,
        out_specs=row(d),
        out_shape=jax.ShapeDtypeStruct((m, d), F32),
        compiler_params=_cparams("arbitrary"),
        name="out_proj",
    )(*ins)


def _ffn_in_seq_kernel(x_ref, nw_ref, wg_ref, wu_ref, cw_ref, cb_ref, act_ref, st_ref, xn_scr, carry_scr, gbuf_scr):
    li = pl.program_id(1)

    @pl.when(li == 0)
    def _():
        carry_scr[...] = jnp.zeros_like(carry_scr)

    xn_scr[...] = _rms(x_ref[0], nw_ref[...]).astype(BF16)
    xn = xn_scr[...]
    tm = xn.shape[0]
    for c0, cw in _col_chunks(wg_ref.shape[1]):
        g = _dot(xn, wg_ref[:, c0:c0 + cw])
        up = _dot(xn, wu_ref[:, c0:c0 + cw])
        y = _conv_rows(g, carry_scr[:, c0:c0 + cw], cw_ref[:, c0:c0 + cw], cb_ref[:, c0:c0 + cw], gbuf_scr)
        act_ref[0, :, c0:c0 + cw] = (_silu(y) * up).astype(act_ref.dtype)
        carry_scr[:, c0:c0 + cw] = g[tm - SUBLANES:]
    st_ref[0] = carry_scr[...]


def _ffn_in_seq(h, nw, wg, wu, conv_w, conv_b, tm):
    bsz, seq, d = h.shape
    nf = wg.shape[1]
    assert seq % tm == 0
    row = lambda n: pl.BlockSpec((1, tm, n), lambda b, l: (b, l, 0))
    return pl.pallas_call(
        _ffn_in_seq_kernel,
        grid=(bsz, seq // tm),
        in_specs=[row(d), _const_spec(nw.shape), _const_spec(wg.shape), _const_spec(wu.shape),
                  _const_spec(conv_w.shape), _const_spec(conv_b.shape)],
        out_specs=[row(nf), pl.BlockSpec((1, SUBLANES, nf), lambda b, l: (b, 0, 0))],
        out_shape=[jax.ShapeDtypeStruct((bsz, seq, nf), BF16),
                   jax.ShapeDtypeStruct((bsz, SUBLANES, nf), F32)],
        scratch_shapes=[pltpu.VMEM((tm, d), BF16), pltpu.VMEM((SUBLANES, nf), F32),
                        pltpu.VMEM((SUBLANES + tm, COL_CHUNK), F32)],
        compiler_params=_cparams("arbitrary", "arbitrary"),
        name="ffn_in_seq",
    )(h, nw, wg, wu, conv_w, conv_b)


def _ffn_in_step_kernel(n_new, x_ref, st_ref, nw_ref, wg_ref, wu_ref, cw_ref, cb_ref, act_ref, nst_ref):
    xn = _rms(x_ref[...], nw_ref[...]).astype(BF16)
    nb = xn.shape[0] // n_new
    width = cw_ref.shape[0]
    for c0, cw in _col_chunks(wg_ref.shape[1]):
        g = _dot(xn, wg_ref[:, c0:c0 + cw])
        up = _dot(xn, wu_ref[:, c0:c0 + cw])
        slabs = [st_ref[k, :, c0:c0 + cw] for k in range(width - 1)]
        slabs += [g[t * nb:(t + 1) * nb] for t in range(n_new)]
        outs = _conv_slabs(slabs, cw_ref[:, c0:c0 + cw], cb_ref[:, c0:c0 + cw], n_new)
        for t in range(n_new):
            act_ref[t * nb:(t + 1) * nb, c0:c0 + cw] = \
                (_silu(outs[t]) * up[t * nb:(t + 1) * nb]).astype(act_ref.dtype)
        for k in range(width - 1):
            nst_ref[k, :, c0:c0 + cw] = slabs[n_new + k]


def _ffn_in_step(x_tm, st_tm, nw, wg, wu, conv_w, conv_b, n_new):
    m, d = x_tm.shape
    nf = wg.shape[1]
    args = (x_tm, st_tm, nw, wg, wu, conv_w, conv_b)
    return pl.pallas_call(
        functools.partial(_ffn_in_step_kernel, n_new),
        grid=(1,),
        in_specs=[_const_spec(a.shape) for a in args],
        out_specs=[_const_spec((m, nf)), _const_spec(st_tm.shape)],
        out_shape=[jax.ShapeDtypeStruct((m, nf), BF16), jax.ShapeDtypeStruct(st_tm.shape, F32)],
        compiler_params=_cparams("arbitrary"),
        name="ffn_in_step",
    )(*args)


def _kvf_kernel(x_ref, nw_ref, wk_ref, wv_ref, wf_ref, bf_ref, tri_ref,
                k_ref, v_ref, kb_ref, vb_ref, lf_ref, fc_ref, xn_scr, carry_scr):
    li = pl.program_id(1)

    @pl.when(li == 0)
    def _():
        carry_scr[...] = jnp.zeros_like(carry_scr)

    xn_scr[...] = _rms(x_ref[0], nw_ref[...]).astype(BF16)
    xn = xn_scr[...]
    for c0, cw in _col_chunks(wk_ref.shape[1]):
        k = _dot(xn, wk_ref[:, c0:c0 + cw])
        k_ref[0, :, c0:c0 + cw] = k
        kb_ref[0, :, c0:c0 + cw] = k.astype(BF16)
        v = _dot(xn, wv_ref[:, c0:c0 + cw])
        v_ref[0, :, c0:c0 + cw] = v
        vb_ref[0, :, c0:c0 + cw] = v.astype(BF16)
    f = _dot(xn, wf_ref[...]) + bf_ref[...]
    lf = -_softplus(-f)
    lf_ref[0] = lf
    fc = _dot3_sel_x(tri_ref[...], lf) + carry_scr[0:1, :]
    fc_ref[0] = fc
    tm = fc.shape[0]
    carry_scr[...] = jnp.broadcast_to(fc[tm - 1:tm, :], carry_scr.shape)


def _kvf(h, nw, wk, wv, wf, bf, tri, tm):
    bsz, seq, d = h.shape
    nk = wk.shape[1]
    assert seq % tm == 0
    row = lambda n: pl.BlockSpec((1, tm, n), lambda b, l: (b, l, 0))
    ins = (h, nw, wk, wv, wf, bf, tri)
    return pl.pallas_call(
        _kvf_kernel,
        grid=(bsz, seq // tm),
        in_specs=[row(d)] + [_const_spec(a.shape) for a in ins[1:]],
        out_specs=[row(nk), row(nk), row(nk), row(nk), row(LANES), row(LANES)],
        out_shape=[jax.ShapeDtypeStruct((bsz, seq, nk), F32), jax.ShapeDtypeStruct((bsz, seq, nk), F32),
                   jax.ShapeDtypeStruct((bsz, seq, nk), BF16), jax.ShapeDtypeStruct((bsz, seq, nk), BF16),
                   jax.ShapeDtypeStruct((bsz, seq, LANES), F32), jax.ShapeDtypeStruct((bsz, seq, LANES), F32)],
        scratch_shapes=[pltpu.VMEM((tm, d), BF16), pltpu.VMEM((SUBLANES, LANES), F32)],
        compiler_params=_cparams("arbitrary", "arbitrary"),
        name="kvf_proj",
    )(*ins)


def _q_proj_kernel(scale, x_ref, nw_ref, w_ref, q_ref):
    xn = _rms(x_ref[...], nw_ref[...]).astype(BF16)
    for c0, cw in _col_chunks(w_ref.shape[1]):
        q_ref[:, c0:c0 + cw] = (_dot(xn, w_ref[:, c0:c0 + cw]) * scale).astype(q_ref.dtype)


def _q_proj(h, nw, w, scale, tm, out_dtype):
    m, d = h.shape
    n = w.shape[1]
    assert m % tm == 0
    row = lambda c: pl.BlockSpec((tm, c), lambda i: (i, 0))
    return pl.pallas_call(
        functools.partial(_q_proj_kernel, scale),
        grid=(m // tm,),
        in_specs=[row(d), _const_spec(nw.shape), _const_spec(w.shape)],
        out_specs=row(n),
        out_shape=jax.ShapeDtypeStruct((m, n), out_dtype),
        compiler_params=_cparams("arbitrary"),
        name="q_proj",
    )(h, nw, w)


LOG2E = 1.4426950408889634
N_SPLIT = 3
KV_BUILD_ROWS = 512


def _bias_selectors(hp, key_side):
    hrow = lax.broadcasted_iota(jnp.int32, (LANES, LANES), 0)
    lcol = lax.broadcasted_iota(jnp.int32, (LANES, LANES), 1)
    off = N_SPLIT if key_side else 0
    sels = []
    for i in range(2):
        per_head = []
        for r in range(N_SPLIT):
            hit = jnp.where(hrow == 2 * hp + i, lcol, -1) == 2 * N_SPLIT * i + off + r
            per_head.append(jnp.where(hit, 1.0, 0.0).astype(BF16))
        sels.append(per_head)
    lane = lax.broadcasted_iota(jnp.int32, (1, LANES), 1)
    ones = []
    for i in range(2):
        lo = 2 * N_SPLIT * i + (0 if key_side else N_SPLIT)
        ones.append(jnp.where((lane >= lo) & (lane < lo + N_SPLIT), 1.0, 0.0))
    return sels, ones


def _fox_prompt_kernel(q_ref, k_ref, v_ref, fc_ref, o_ref, kx_scr, qx_scr, sa_scr, sb_scr, m0_scr, m1_scr, l_scr,
                       acc_scr):
    hp = pl.program_id(1)
    u = pl.program_id(2)
    tq = q_ref.shape[1] // 2
    tk = tq
    seq = k_ref.shape[1]
    lane = lax.broadcasted_iota(jnp.int32, (tq, LANES), 1)
    first = lane < ATT_HEAD_DIM

    @pl.when(u == 0)
    def _():
        sels, ones = _bias_selectors(hp, True)
        both = [sels[0][r] + sels[1][r] for r in range(N_SPLIT)]
        one_row = ones[0] + ones[1]

        def build(c, carry):
            r0 = pl.multiple_of(c * KV_BUILD_ROWS, KV_BUILD_ROWS)
            parts = _split3(-(fc_ref[0, pl.ds(r0, KV_BUILD_ROWS), :] * LOG2E))
            ext = one_row + _dot(parts[0], both[0])
            for r in range(1, N_SPLIT):
                ext = ext + _dot(parts[r], both[r])
            kx_scr[pl.ds(r0, KV_BUILD_ROWS), 0:LANES] = k_ref[0, pl.ds(r0, KV_BUILD_ROWS), :]
            kx_scr[pl.ds(r0, KV_BUILD_ROWS), LANES:2 * LANES] = ext.astype(BF16)
            return carry

        lax.fori_loop(0, seq // KV_BUILD_ROWS, build, 0)

    sels, ones = _bias_selectors(hp, False)

    def build_qx(t):
        q = q_ref[0, t * tq:(t + 1) * tq, :]
        zq = jnp.zeros_like(q)
        q0 = pl.multiple_of((2 * u + t) * tq, tq)
        fparts = _split3(fc_ref[0, pl.ds(q0, tq), :] * LOG2E)
        for i in range(2):
            ext = ones[i] + _dot(fparts[0], sels[i][0])
            for r in range(1, N_SPLIT):
                ext = ext + _dot(fparts[r], sels[i][r])
            qh = jnp.where(first, q, zq) if i == 0 else jnp.where(first, zq, q)
            qx_scr[t, i] = jnp.concatenate([qh, ext.astype(BF16)], axis=1)

    def reset_state():
        m0_scr[...] = jnp.full_like(m0_scr, -jnp.inf)
        m1_scr[...] = jnp.full_like(m1_scr, -jnp.inf)
        l_scr[...] = jnp.zeros_like(l_scr)
        acc_scr[...] = jnp.zeros_like(acc_scr)

    def finish(t):
        o_ref[0, t * tq:(t + 1) * tq, :] = (acc_scr[...] / l_scr[...]).astype(o_ref.dtype)

    first_k = lax.broadcasted_iota(jnp.int32, (tk, LANES), 1) < ATT_HEAD_DIM
    ones_top = jnp.where(first_k, 1.0, 0.0).astype(BF16)
    ones_bot = jnp.where(first_k, 0.0, 1.0).astype(BF16)
    m_scrs = (m0_scr, m1_scr)

    def qk_into(buf, t, kj):
        start = pl.multiple_of(kj * tk, tk)
        kx = kx_scr[pl.ds(start, tk), :]
        for i in range(2):
            buf[i * tq:(i + 1) * tq, :] = _dot_nt(qx_scr[t, i], kx)

    def consume(buf, kj, masked):
        start = pl.multiple_of(kj * tk, tk)
        v = v_ref[0, pl.ds(start, tk), :]
        zv = jnp.zeros_like(v)
        vx = jnp.concatenate([jnp.concatenate([jnp.where(first_k, v, zv), ones_top], axis=1),
                              jnp.concatenate([jnp.where(first_k, zv, v), ones_bot], axis=1)], axis=0)
        ps, alphas = [], []
        for i in range(2):
            chunks = [buf[i * tq:(i + 1) * tq, c0:c0 + LANES] for c0 in range(0, tk, LANES)]
            if masked:
                r = lax.broadcasted_iota(jnp.int32, (tq, LANES), 0)
                c = lax.broadcasted_iota(jnp.int32, (tq, LANES), 1)
                chunks = [jnp.where(r >= c + c0, ch, -jnp.inf) for ch, c0 in zip(chunks, range(0, tk, LANES))]
            m_old = m_scrs[i][...]
            m_new = jnp.maximum(m_old, jnp.max(functools.reduce(jnp.maximum, chunks), axis=1, keepdims=True))
            alphas.append(jnp.exp2(m_old - m_new))
            ps.append(jnp.concatenate([jnp.exp2(ch - m_new).astype(BF16) for ch in chunks], axis=1))
            m_scrs[i][...] = m_new
        tot = _dot(jnp.concatenate(ps, axis=1), vx)
        alpha = jnp.where(first, alphas[0], alphas[1])
        acc_scr[...] = alpha * acc_scr[...] + tot[:, :LANES]
        l_scr[...] = alpha * l_scr[...] + tot[:, LANES:]

    def pair_loop(t, cur, nxt):
        def body(j, c):
            qk_into(nxt, t, 2 * j + 1)
            consume(cur, 2 * j, False)
            qk_into(cur, t, 2 * j + 2)
            consume(nxt, 2 * j + 1, False)
            return c
        lax.fori_loop(0, u, body, 0)

    build_qx(0)
    reset_state()
    qk_into(sa_scr, 0, 0)
    pair_loop(0, sa_scr, sb_scr)
    build_qx(1)
    qk_into(sb_scr, 1, 0)
    consume(sa_scr, 2 * u, True)
    finish(0)
    reset_state()
    pair_loop(1, sb_scr, sa_scr)
    qk_into(sa_scr, 1, 2 * u + 1)
    consume(sb_scr, 2 * u, False)
    consume(sa_scr, 2 * u + 1, True)
    finish(1)


def _fox_prompt(q, kb, vb, fc, tq):
    bsz, seq, dm = q.shape
    n_pairs = dm // LANES
    assert seq % (2 * tq) == 0 and seq % KV_BUILD_ROWS == 0
    n_steps = seq // (2 * tq)
    return pl.pallas_call(
        _fox_prompt_kernel,
        grid=(bsz, n_pairs, n_steps),
        in_specs=[pl.BlockSpec((1, 2 * tq, LANES), lambda b, p, i: (b, i, p)),
                  pl.BlockSpec((1, seq, LANES), lambda b, p, i: (b, 0, p)),
                  pl.BlockSpec((1, seq, LANES), lambda b, p, i: (b, 0, p)),
                  pl.BlockSpec((1, seq, LANES), lambda b, p, i: (b, 0, 0))],
        out_specs=pl.BlockSpec((1, 2 * tq, LANES), lambda b, p, i: (b, i, p)),
        out_shape=jax.ShapeDtypeStruct((bsz, seq, dm), BF16),
        scratch_shapes=[pltpu.VMEM((seq, 2 * LANES), BF16), pltpu.VMEM((2, 2, tq, 2 * LANES), BF16),
                        pltpu.VMEM((2 * tq, tq), F32), pltpu.VMEM((2 * tq, tq), F32),
                        pltpu.VMEM((tq, LANES), F32), pltpu.VMEM((tq, LANES), F32),
                        pltpu.VMEM((tq, LANES), F32), pltpu.VMEM((tq, LANES), F32)],
        compiler_params=_cparams("arbitrary", "arbitrary", "arbitrary"),
        name="fox_prompt",
    )(q, kb, vb, fc)


def _fox_sample_kernel(n_group, n_new, pt_ref, q_ref, kn_ref, vn_ref, lfn_ref, suf_ref, *refs):
    k_refs = refs[0:n_group]
    v_refs = refs[n_group:2 * n_group]
    lf_refs = refs[2 * n_group:3 * n_group]
    o_ref, m_scr, l_scr, acc_scr, fn_scr, carry_scr = refs[3 * n_group:]
    i = pl.program_id(1)
    n_heads, n_rows, dh = q_ref.shape[1:]
    page = suf_ref.shape[0]

    @pl.when(i == 0)
    def _():
        y = lfn_ref[0]
        t3 = lax.broadcasted_iota(jnp.int32, y.shape, 1)
        sh = 1
        while sh < n_new:
            y = y + jnp.where(t3 >= sh, pltpu.roll(y, sh, 1), 0.0)
            sh *= 2
        fn_scr[...] = y
        q = q_ref[0]
        t1 = lax.broadcasted_iota(jnp.int32, (n_heads, n_rows, 1), 1)
        fcol = y[:, :, 0:1]
        ss = []
        for u in range(n_new):
            s_u = jnp.sum(q * kn_ref[0, :, u:u + 1, :], axis=2, keepdims=True) + fcol - fcol[:, u:u + 1, :]
            ss.append(jnp.where(t1 >= u, s_u, -jnp.inf))
        m0 = functools.reduce(jnp.maximum, ss)
        l0 = jnp.zeros_like(m0)
        acc0 = jnp.zeros((n_heads, n_rows, dh), F32)
        for u in range(n_new):
            p_u = jnp.exp(ss[u] - m0)
            l0 = l0 + p_u
            acc0 = acc0 + p_u * vn_ref[0, :, u:u + 1, :]
        m_scr[...] = jnp.broadcast_to(m0, m_scr.shape)
        l_scr[...] = jnp.broadcast_to(l0, l_scr.shape)
        acc_scr[...] = acc0
        carry_scr[...] = jnp.zeros_like(carry_scr)

    suf = suf_ref[...]
    withins, prefixes = [], []
    run = carry_scr[:, 0:1]
    for g in range(n_group):
        lf_t = lf_refs[g][0]
        withins.append(_dot3_x_sel(lf_t, suf))
        prefixes.append(run)
        run = run + jnp.sum(lf_t, axis=1, keepdims=True)
    carry_scr[...] = jnp.broadcast_to(run, carry_scr.shape)

    shift = jnp.concatenate([withins[g] + prefixes[g] for g in range(n_group)], axis=1)
    fn = jnp.concatenate([fn_scr[...]] * n_group, axis=2)
    s3 = jnp.stack([
        _dot(q_ref[0, h].astype(BF16),
             jnp.concatenate([k_refs[g][0, h] for g in range(n_group)], axis=1).astype(BF16)) + shift[h:h + 1, :]
        for h in range(n_heads)]) + fn
    m_old = m_scr[...]
    m_new = jnp.maximum(m_old, jnp.max(s3, axis=2, keepdims=True))
    alpha = jnp.exp(m_old - m_new)
    p3 = jnp.exp(s3 - jnp.concatenate([m_new] * n_group, axis=2))
    l_scr[...] = alpha * l_scr[...] + jnp.sum(p3, axis=2, keepdims=True)
    m_scr[...] = m_new
    pb = p3.astype(BF16)
    pv = jnp.stack([
        _dot_nt(pb[h], jnp.concatenate([v_refs[g][0, h] for g in range(n_group)], axis=1).astype(BF16))
        for h in range(n_heads)])
    acc_scr[...] = alpha[:, :, 0:dh] * acc_scr[...] + pv

    @pl.when(i == pl.num_programs(1) - 1)
    def _():
        o_ref[0] = acc_scr[...] / l_scr[:, :, 0:dh]


def _fox_sample(q4, kn4, vn4, lfn4, cache_k, cache_v, cache_lf_t, page_table, suf, n_group, n_new):
    db, n_heads, n_rows, dh = q4.shape
    n_pages = page_table.shape[1]
    page = cache_lf_t.shape[2]
    assert n_pages % n_group == 0 and page == LANES and n_new <= n_rows
    pt_flat = page_table.reshape(-1)

    def page_map(g, ndim):
        def imap(b, i, pt):
            return (pt[b * n_pages + (n_pages - 1 - (i * n_group + g))],) + (0,) * (ndim - 1)
        return imap

    seq_spec = lambda c: pl.BlockSpec((1, n_heads, n_rows, c), lambda b, i, pt: (b, 0, 0, 0))
    in_specs = [seq_spec(dh), seq_spec(dh), seq_spec(dh), seq_spec(LANES),
                pl.BlockSpec(suf.shape, lambda b, i, pt: (0, 0))]
    in_specs += [pl.BlockSpec((1, n_heads, dh, page), page_map(g, 4)) for g in range(n_group)]
    in_specs += [pl.BlockSpec((1, n_heads, dh, page), page_map(g, 4)) for g in range(n_group)]
    in_specs += [pl.BlockSpec((1, n_heads, page), page_map(g, 3)) for g in range(n_group)]
    grid_spec = pltpu.PrefetchScalarGridSpec(
        num_scalar_prefetch=1,
        grid=(db, n_pages // n_group),
        in_specs=in_specs,
        out_specs=seq_spec(dh),
        scratch_shapes=[pltpu.VMEM((n_heads, n_rows, LANES), F32), pltpu.VMEM((n_heads, n_rows, LANES), F32),
                        pltpu.VMEM((n_heads, n_rows, dh), F32), pltpu.VMEM((n_heads, n_rows, LANES), F32),
                        pltpu.VMEM((n_heads, LANES), F32)],
    )
    return pl.pallas_call(
        functools.partial(_fox_sample_kernel, n_group, n_new),
        grid_spec=grid_spec,
        out_shape=jax.ShapeDtypeStruct((db, n_heads, n_rows, dh), F32),
        compiler_params=_cparams("arbitrary", "arbitrary"),
        name="fox_sample",
    )(pt_flat, q4, kn4, vn4, lfn4, suf, *([cache_k] * n_group), *([cache_v] * n_group),
      *([cache_lf_t] * n_group))


def _prep_weights(norm_w, m_w_in, m_conv_w, m_conv_b, m_dt_bias, m_a_log, m_d, m_gnorm_w, m_w_out,
                  ffn_w_in, ffn_conv_w, ffn_conv_b, ffn_w_out, kv_norm_w, w_kvf, b_f, w_q, w_o):
    d_model = norm_w.shape[-1]
    d_inner = m_w_out.shape[1]
    n_sheads = m_dt_bias.shape[1]
    conv_dim = m_conv_w.shape[2]
    d_ff = ffn_w_out.shape[1]
    att_dim = w_q.shape[2]
    n_aheads = b_f.shape[0]
    pad_lanes = lambda a: jnp.pad(a, [(0, 0)] * (a.ndim - 1) + [(0, LANES - a.shape[-1])])
    w = {}
    w["norm"] = norm_w.reshape(norm_w.shape[0], norm_w.shape[1], 1, d_model)
    w["m_wz"] = m_w_in[:, :, :d_inner].astype(BF16)
    w["m_wx"] = m_w_in[:, :, d_inner:d_inner + conv_dim].astype(BF16)
    w["m_wdt"] = pad_lanes(m_w_in[:, :, d_inner + conv_dim:]).astype(BF16)
    w["m_conv_w"] = m_conv_w
    w["m_conv_b"] = m_conv_b[:, None, :]
    w["m_dtb"] = pad_lanes(m_dt_bias)[:, None, :]
    w["m_alog"] = pad_lanes(m_a_log)[:, None, :]
    w["m_dsk"] = jnp.repeat(m_d.astype(F32), SSM_HEAD_DIM, axis=1)[:, None, :]
    w["m_gw"] = m_gnorm_w[:, None, :]
    w["m_wout"] = m_w_out.astype(BF16)
    w["f_wg"] = ffn_w_in[:, :, :d_ff].astype(BF16)
    w["f_wu"] = ffn_w_in[:, :, d_ff:].astype(BF16)
    w["f_conv_w"] = ffn_conv_w
    w["f_conv_b"] = ffn_conv_b[:, None, :]
    w["f_wout"] = ffn_w_out.astype(BF16)
    w["kv_norm"] = kv_norm_w[None, :]
    w["wk"] = w_kvf[:, :att_dim].astype(BF16)
    w["wv"] = w_kvf[:, att_dim:2 * att_dim].astype(BF16)
    w["wf"] = pad_lanes(w_kvf[:, 2 * att_dim:]).astype(BF16)
    w["bf"] = pad_lanes(b_f[None, :])
    w["wq"] = w_q.astype(BF16)
    w["wo"] = w_o.astype(BF16)
    hh = jnp.arange(LANES)[:, None]
    cc = jnp.arange(d_inner)[None, :] // SSM_HEAD_DIM
    w["ex"] = (hh == cc).astype(BF16)
    r = jnp.arange(SSD_CHUNK)
    w["tri_chunk"] = (r[:, None] >= r[None, :]).astype(BF16)
    p = jnp.arange(PAGE_SIZE)
    w["suf"] = (p[:, None] > p[None, :]).astype(BF16)
    w["n_sheads"] = n_sheads
    w["n_aheads"] = n_aheads
    return w


TM_SEQ = 1024
TM_MAMBA_IN = 512
TM_KVF = 512
SSD_SUB = 4
TQ = 512
N_PAGE_GROUP = 16


def _tri(n):
    r = jnp.arange(n)
    return (r[:, None] >= r[None, :]).astype(BF16)


def _prompt_trunk(x, w, depth, n_a):
    bsz, seq, d = x.shape
    m = bsz * seq
    h = x
    d_inner = w["m_wout"].shape[1]
    outs_ssm, outs_mconv, outs_ffn = [], [], []
    kf = vf = kb = vb = lf = fc = None
    for i in range(depth):
        nw = w["norm"][i]
        if i < n_a:
            z, xa, dt, cst = _mamba_in_seq(h, nw[0], w["m_wz"][i], w["m_wx"][i], w["m_wdt"][i],
                                           w["m_conv_w"][i], w["m_conv_b"][i], w["m_dtb"][i], TM_MAMBA_IN)
            s0 = jnp.zeros((bsz, d_inner, D_STATE), F32)
            y, s_new = _ssd(xa, z, dt, s0, w["m_alog"][i], w["m_dsk"][i], w["ex"], w["tri_chunk"], SSD_SUB)
            h = _out_proj(y.reshape(m, d_inner), w["m_wout"][i], h.reshape(m, d), nw[1], TM_SEQ,
                          gw=w["m_gw"][i]).reshape(bsz, seq, d)
            width = w["m_conv_w"].shape[1]
            outs_mconv.append(cst[:, SUBLANES - (width - 1):, :])
            outs_ssm.append(s_new.reshape(bsz, w["n_sheads"], SSM_HEAD_DIM, D_STATE))
        else:
            j = i - n_a
            q = _q_proj(h.reshape(m, d), nw[0], w["wq"][j], ATT_HEAD_DIM ** -0.5 * LOG2E, TM_SEQ, BF16)
            o = _fox_prompt(q.reshape(bsz, seq, -1), kb, vb, fc, TQ)
            h = _out_proj(o.reshape(m, -1), w["wo"][j], h.reshape(m, d), nw[1], TM_SEQ).reshape(bsz, seq, d)
        act, fst = _ffn_in_seq(h, nw[2], w["f_wg"][i], w["f_wu"][i], w["f_conv_w"][i], w["f_conv_b"][i], TM_SEQ)
        h = _out_proj(act.reshape(m, -1), w["f_wout"][i], h.reshape(m, d), nw[3], TM_SEQ).reshape(bsz, seq, d)
        fwidth = w["f_conv_w"].shape[1]
        outs_ffn.append(fst[:, SUBLANES - (fwidth - 1):, :])
        if i == n_a - 1:
            kf, vf, kb, vb, lf, fc = _kvf(h, w["kv_norm"], w["wk"], w["wv"], w["wf"], w["bf"], _tri(TM_KVF), TM_KVF)
    n_ah = w["n_aheads"]
    return (h, jnp.stack(outs_ssm), jnp.stack(outs_mconv), jnp.stack(outs_ffn),
            kf.reshape(bsz, seq, n_ah, ATT_HEAD_DIM), vf.reshape(bsz, seq, n_ah, ATT_HEAD_DIM), lf[:, :, :n_ah])


def _to_tm(a):
    db, n_new, c = a.shape
    return jnp.swapaxes(a, 0, 1).reshape(n_new * db, c)


def _from_tm(a, db):
    m, c = a.shape
    return jnp.swapaxes(a.reshape(m // db, db, c), 0, 1)


def _heads_major(a, n_heads):
    db, n_new, c = a.shape
    a4 = jnp.swapaxes(a.reshape(db, n_new, n_heads, c // n_heads), 1, 2)
    return jnp.pad(a4, ((0, 0), (0, 0), (0, SUBLANES - n_new), (0, 0)))


def _sample_trunk(x, ssm_st, mconv_st, ffn_st, cache_k, cache_v, cache_logf, page_table, w, depth, n_a):
    db, n_new, d = x.shape
    m = db * n_new
    h = _to_tm(x)
    d_inner = w["m_wout"].shape[1]
    n_ah = w["n_aheads"]
    outs_ssm, outs_mconv, outs_ffn = [], [], []
    kf = vf = lf = None
    for i in range(depth):
        nw = w["norm"][i]
        if i < n_a:
            st_tm = jnp.swapaxes(mconv_st[i], 0, 1)
            z, xa, dt, nst = _mamba_in_step(h, st_tm, nw[0], w["m_wz"][i], w["m_wx"][i], w["m_wdt"][i],
                                            w["m_conv_w"][i], w["m_conv_b"][i], w["m_dtb"][i], n_new)
            s0 = ssm_st[i].reshape(db, d_inner, D_STATE)
            y, s_new = _ssd(_from_tm(xa, db), _from_tm(z, db), _from_tm(dt, db), s0,
                            w["m_alog"][i], w["m_dsk"][i], w["ex"], w["tri_chunk"], 1)
            y_tm = _to_tm(y)
            h = _out_proj(y_tm, w["m_wout"][i], h, nw[1], m, gw=w["m_gw"][i])
            outs_mconv.append(jnp.swapaxes(nst, 0, 1))
            outs_ssm.append(s_new.reshape(db, w["n_sheads"], SSM_HEAD_DIM, D_STATE))
        else:
            j = i - n_a
            q = _from_tm(_q_proj(h, nw[0], w["wq"][j], ATT_HEAD_DIM ** -0.5, m, F32), db)
            ck_t = jnp.transpose(cache_k, (0, 2, 3, 1))
            cv_t = jnp.transpose(cache_v, (0, 2, 3, 1))
            clf_t = jnp.swapaxes(cache_logf, 1, 2)
            lfn3 = jnp.pad(jnp.swapaxes(lf[:, :, :n_ah], 1, 2), ((0, 0), (0, 0), (0, SUBLANES - n_new)))
            lfn4 = jnp.broadcast_to(lfn3[..., None], (db, n_ah, SUBLANES, LANES))
            o4 = _fox_sample(_heads_major(q, n_ah), _heads_major(kf, n_ah), _heads_major(vf, n_ah), lfn4,
                             ck_t, cv_t, clf_t, page_table, w["suf"], N_PAGE_GROUP, n_new)
            o = jnp.swapaxes(o4[:, :, :n_new, :], 1, 2).reshape(db, n_new, -1)
            h = _out_proj(_to_tm(o), w["wo"][j], h, nw[1], m)
        fst_tm = jnp.swapaxes(ffn_st[i], 0, 1)
        act, nfst = _ffn_in_step(h, fst_tm, nw[2], w["f_wg"][i], w["f_wu"][i], w["f_conv_w"][i],
                                 w["f_conv_b"][i], n_new)
        h = _out_proj(act, w["f_wout"][i], h, nw[3], m)
        outs_ffn.append(jnp.swapaxes(nfst, 0, 1))
        if i == n_a - 1:
            kf_tm, vf_tm, _, _, lf_tm, _ = _kvf(h[None], w["kv_norm"], w["wk"], w["wv"], w["wf"], w["bf"],
                                                _tri(m), m)
            kf, vf, lf = _from_tm(kf_tm[0], db), _from_tm(vf_tm[0], db), _from_tm(lf_tm[0], db)
    return (_from_tm(h, db), jnp.stack(outs_ssm), jnp.stack(outs_mconv), jnp.stack(outs_ffn),
            kf.reshape(db, n_new, n_ah, ATT_HEAD_DIM), vf.reshape(db, n_new, n_ah, ATT_HEAD_DIM), lf[:, :, :n_ah])


def kernel(x_prompt, x_sample, state_ssm, state_mconv, state_ffn, cache_k, cache_v, cache_logf, page_table,
           norm_w, m_w_in, m_conv_w, m_conv_b, m_dt_bias, m_a_log, m_d, m_gnorm_w, m_w_out,
           ffn_w_in, ffn_conv_w, ffn_conv_b, ffn_w_out, kv_norm_w, w_kvf, b_f, w_q, w_o):
    depth = norm_w.shape[0]
    n_a = m_w_in.shape[0]
    w = _prep_weights(norm_w, m_w_in, m_conv_w, m_conv_b, m_dt_bias, m_a_log, m_d, m_gnorm_w, m_w_out,
                      ffn_w_in, ffn_conv_w, ffn_conv_b, ffn_w_out, kv_norm_w, w_kvf, b_f, w_q, w_o)
    y_p, p_ssm, p_mconv, p_ffn, p_k, p_v, p_lf = _prompt_trunk(x_prompt, w, depth, n_a)
    y_s, s_ssm, s_mconv, s_ffn, s_k, s_v, s_lf = _sample_trunk(
        x_sample, state_ssm, state_mconv, state_ffn, cache_k, cache_v, cache_logf, page_table, w, depth, n_a)
    return (y_p, y_s, p_ssm, p_mconv, p_ffn, p_k, p_v, p_lf, s_ssm, s_mconv, s_ffn, s_k, s_v, s_lf)
```

```python
import functools
import math

import jax
import jax.numpy as jnp
from jax import lax
from jax.experimental import pallas as pl
from jax.experimental.pallas import tpu as pltpu

F32 = jnp.float32
BF16 = jnp.bfloat16
EPS = 1e-6

LANES = 128
SUBLANES = 8
VMEM_LIMIT_BYTES = 56 * 1024 * 1024

SSM_HEAD_DIM = 64
SSM_GROUPS = 4
D_STATE = 128
SSD_CHUNK = 128
ATT_HEAD_DIM = 64
PAGE_SIZE = 128


def _cparams(*sem):
    return pltpu.CompilerParams(dimension_semantics=sem, vmem_limit_bytes=VMEM_LIMIT_BYTES)


def _const_spec(shape):
    nd = len(shape)
    return pl.BlockSpec(shape, lambda *_: (0,) * nd)


def _rms(x, w):
    ms = jnp.mean(x * x, axis=-1, keepdims=True)
    return x * lax.rsqrt(ms + EPS) * w


def _silu(x):
    return x / (1.0 + jnp.exp(-x))


def _softplus(x):
    return jnp.maximum(x, 0.0) + jnp.log1p(jnp.exp(-jnp.abs(x)))


def _dot(a, b):
    return jnp.dot(a, b, preferred_element_type=F32)


def _dot_nt(a, b):
    return lax.dot_general(a, b, (((1,), (1,)), ((), ())), preferred_element_type=F32)


def _dot_tn(a, b):
    return lax.dot_general(a, b, (((0,), (0,)), ((), ())), preferred_element_type=F32)


def _split3(x):
    hi = x.astype(BF16)
    r1 = x - hi.astype(F32)
    mid = r1.astype(BF16)
    lo = (r1 - mid.astype(F32)).astype(BF16)
    return hi, mid, lo


def _dot2_x_sel(x, sel):
    hi, mid, _ = _split3(x)
    return _dot(hi, sel) + _dot(mid, sel)


def _dot3_x_sel(x, sel):
    hi, mid, lo = _split3(x)
    return _dot(hi, sel) + _dot(mid, sel) + _dot(lo, sel)


def _dot3_sel_x(sel, x):
    hi, mid, lo = _split3(x)
    return _dot(sel, hi) + _dot(sel, mid) + _dot(sel, lo)


def _conv_rows(g, carry8, w, b, gbuf):
    tm, n = g.shape
    width = w.shape[0]
    gbuf[0:SUBLANES, 0:n] = carry8
    gbuf[SUBLANES:SUBLANES + tm, 0:n] = g
    out = b + g * w[width - 1:width]
    for k in range(width - 1):
        start = SUBLANES - (width - 1 - k)
        out = out + gbuf[start:start + tm, 0:n] * w[k:k + 1]
    return out


def _conv_slabs(slabs, w, b, n_new):
    width = w.shape[0]
    outs = []
    for t in range(n_new):
        o = b + slabs[t] * w[0:1]
        for k in range(1, width):
            o = o + slabs[t + k] * w[k:k + 1]
        outs.append(o)
    return outs


COL_CHUNK = 512


def _col_chunks(n, step=COL_CHUNK):
    out, c = [], 0
    while c < n:
        w = min(step, n - c)
        out.append((c, w))
        c += w
    return out


def _mamba_in_seq_kernel(x_ref, nw_ref, wz_ref, wx_ref, wdt_ref, cw_ref, cb_ref, dtb_ref,
                         z_ref, xa_ref, dt_ref, st_ref, xn_scr, carry_scr, gbuf_scr):
    li = pl.program_id(1)

    @pl.when(li == 0)
    def _():
        carry_scr[...] = jnp.zeros_like(carry_scr)

    xn_scr[...] = _rms(x_ref[0], nw_ref[...]).astype(BF16)
    xn = xn_scr[...]
    tm = xn.shape[0]
    for c0, cw in _col_chunks(wz_ref.shape[1]):
        z_ref[0, :, c0:c0 + cw] = _dot(xn, wz_ref[:, c0:c0 + cw])
    for c0, cw in _col_chunks(wx_ref.shape[1]):
        g = _dot(xn, wx_ref[:, c0:c0 + cw])
        carry8 = carry_scr[:, c0:c0 + cw]
        y = _conv_rows(g, carry8, cw_ref[:, c0:c0 + cw], cb_ref[:, c0:c0 + cw], gbuf_scr)
        xa_ref[0, :, c0:c0 + cw] = _silu(y)
        carry_scr[:, c0:c0 + cw] = g[tm - SUBLANES:]
    dt_ref[0] = _softplus(_dot(xn, wdt_ref[...]) + dtb_ref[...])
    st_ref[0] = carry_scr[...]


def _mamba_in_seq(h, nw, wz, wx, wdt, conv_w, conv_b, dtb, tm):
    bsz, seq, d = h.shape
    nz, nx = wz.shape[1], wx.shape[1]
    assert seq % tm == 0
    grid = (bsz, seq // tm)
    row = lambda n: pl.BlockSpec((1, tm, n), lambda b, l: (b, l, 0))
    return pl.pallas_call(
        _mamba_in_seq_kernel,
        grid=grid,
        in_specs=[row(d), _const_spec(nw.shape), _const_spec(wz.shape), _const_spec(wx.shape),
                  _const_spec(wdt.shape), _const_spec(conv_w.shape), _const_spec(conv_b.shape),
                  _const_spec(dtb.shape)],
        out_specs=[row(nz), row(nx), row(LANES),
                   pl.BlockSpec((1, SUBLANES, nx), lambda b, l: (b, 0, 0))],
        out_shape=[jax.ShapeDtypeStruct((bsz, seq, nz), F32),
                   jax.ShapeDtypeStruct((bsz, seq, nx), F32),
                   jax.ShapeDtypeStruct((bsz, seq, LANES), F32),
                   jax.ShapeDtypeStruct((bsz, SUBLANES, nx), F32)],
        scratch_shapes=[pltpu.VMEM((tm, d), BF16), pltpu.VMEM((SUBLANES, nx), F32),
                        pltpu.VMEM((SUBLANES + tm, COL_CHUNK), F32)],
        compiler_params=_cparams("arbitrary", "arbitrary"),
        name="mamba_in_seq",
    )(h, nw, wz, wx, wdt, conv_w, conv_b, dtb)


def _mamba_in_step_kernel(n_new, x_ref, st_ref, nw_ref, wz_ref, wx_ref, wdt_ref, cw_ref, cb_ref, dtb_ref,
                          z_ref, xa_ref, dt_ref, nst_ref):
    xn = _rms(x_ref[...], nw_ref[...]).astype(BF16)
    nb = xn.shape[0] // n_new
    width = cw_ref.shape[0]
    for c0, cw in _col_chunks(wz_ref.shape[1]):
        z_ref[:, c0:c0 + cw] = _dot(xn, wz_ref[:, c0:c0 + cw])
    for c0, cw in _col_chunks(wx_ref.shape[1]):
        g = _dot(xn, wx_ref[:, c0:c0 + cw])
        slabs = [st_ref[k, :, c0:c0 + cw] for k in range(width - 1)]
        slabs += [g[t * nb:(t + 1) * nb] for t in range(n_new)]
        outs = _conv_slabs(slabs, cw_ref[:, c0:c0 + cw], cb_ref[:, c0:c0 + cw], n_new)
        for t in range(n_new):
            xa_ref[t * nb:(t + 1) * nb, c0:c0 + cw] = _silu(outs[t])
        for k in range(width - 1):
            nst_ref[k, :, c0:c0 + cw] = slabs[n_new + k]
    dt_ref[...] = _softplus(_dot(xn, wdt_ref[...]) + dtb_ref[...])


def _mamba_in_step(x_tm, st_tm, nw, wz, wx, wdt, conv_w, conv_b, dtb, n_new):
    m, d = x_tm.shape
    nz, nx = wz.shape[1], wx.shape[1]
    args = (x_tm, st_tm, nw, wz, wx, wdt, conv_w, conv_b, dtb)
    return pl.pallas_call(
        functools.partial(_mamba_in_step_kernel, n_new),
        grid=(1,),
        in_specs=[_const_spec(a.shape) for a in args],
        out_specs=[_const_spec((m, nz)), _const_spec((m, nx)), _const_spec((m, LANES)),
                   _const_spec(st_tm.shape)],
        out_shape=[jax.ShapeDtypeStruct((m, nz), F32), jax.ShapeDtypeStruct((m, nx), F32),
                   jax.ShapeDtypeStruct((m, LANES), F32), jax.ShapeDtypeStruct(st_tm.shape, F32)],
        compiler_params=_cparams("arbitrary"),
        name="mamba_in_step",
    )(*args)


def _ssd_kernel(valid, n_sub, xa_ref, z_ref, dt_ref, s0_ref, alog_ref, dsk_ref, ex_ref, tri_ref,
                y_ref, sout_ref, st_scr, *pad_scrs):
    ci = pl.program_id(1)
    d_inner = z_ref.shape[2]
    if pad_scrs:
        t_rows = pad_scrs[0].shape[0]

        @pl.when((pl.program_id(0) == 0) & (ci == 0))
        def _():
            for p in pad_scrs:
                p[...] = jnp.zeros_like(p)

        for p, r in zip(pad_scrs, (xa_ref, z_ref, dt_ref)):
            p[0:valid, :] = r[0]
        xa_at = lambda rs, cs: pad_scrs[0][rs, cs]
        z_at = lambda rs, cs: pad_scrs[1][rs, cs]
        dt_at = lambda rs: pad_scrs[2][rs, :]
    else:
        t_rows = xa_ref.shape[1] // n_sub
        xa_at = lambda rs, cs: xa_ref[0, rs, cs]
        z_at = lambda rs, cs: z_ref[0, rs, cs]
        dt_at = lambda rs: dt_ref[0, rs, :]
    gn = SSM_GROUPS * D_STATE
    n_heads = d_inner // SSM_HEAD_DIM
    heads_per_group = n_heads // SSM_GROUPS
    gcols = heads_per_group * SSM_HEAD_DIM

    @pl.when(ci == 0)
    def _():
        for c0 in range(0, d_inner, LANES):
            st_scr[:, c0:c0 + LANES] = s0_ref[0, c0:c0 + LANES, :].T

    li = lax.broadcasted_iota(jnp.int32, (t_rows, t_rows), 0)
    si = lax.broadcasted_iota(jnp.int32, (t_rows, t_rows), 1)
    causal = li >= si
    lane = lax.broadcasted_iota(jnp.int32, (t_rows, LANES), 1)
    first_head = lane < SSM_HEAD_DIM
    ex = ex_ref[...]
    a_row = -jnp.exp(alog_ref[...])

    for sub in range(n_sub):
        rs = slice(sub * t_rows, (sub + 1) * t_rows)
        xs = xa_at(rs, slice(0, d_inner))
        bm = xa_at(rs, slice(d_inner, d_inner + gn)).astype(BF16)
        cm = xa_at(rs, slice(d_inner + gn, d_inner + 2 * gn)).astype(BF16)
        dt = dt_at(rs)
        if valid < t_rows:
            dt = jnp.where(lax.broadcasted_iota(jnp.int32, dt.shape, 0) < valid, dt, 0.0)
        dta = dt * a_row
        acum = _dot3_sel_x(tri_ref[...], dta)
        acum_t = acum.T
        dt_t = dt.T
        e_acum = _dot2_x_sel(jnp.exp(acum), ex)
        w_end = _dot2_x_sel(dt * jnp.exp(acum[t_rows - 1:t_rows, :] - acum), ex)
        xs_b = xs.astype(BF16)
        xde_b = (xs * w_end).astype(BF16)
        e_last = e_acum[t_rows - 1:t_rows, :]

        for g in range(SSM_GROUPS):
            c_g = cm[:, g * D_STATE:(g + 1) * D_STATE]
            b_g = bm[:, g * D_STATE:(g + 1) * D_STATE]
            cb = _dot_nt(c_g, b_g)
            g0 = g * gcols
            pieces = []
            for k in range(heads_per_group // 2):
                h1 = g * heads_per_group + 2 * k
                ms = []
                for hh in (h1, h1 + 1):
                    seg = acum[:, hh:hh + 1] - acum_t[hh:hh + 1, :]
                    dec = jnp.exp(jnp.where(causal, seg, -jnp.inf))
                    ms.append((cb * dec * dt_t[hh:hh + 1, :]).astype(BF16))
                x2 = xs_b[:, g0 + k * LANES:g0 + (k + 1) * LANES]
                zero = jnp.zeros_like(x2)
                rhs = jnp.concatenate([jnp.where(first_head, x2, zero), jnp.where(first_head, zero, x2)], axis=0)
                pieces.append(_dot(jnp.concatenate(ms, axis=1), rhs))
            y_diag = jnp.concatenate(pieces, axis=1)
            st_g = st_scr[:, g0:g0 + gcols]
            y_off = _dot(c_g, st_g.astype(BF16)) * e_acum[:, g0:g0 + gcols]
            xs_g = xs[:, g0:g0 + gcols]
            y = y_diag + y_off + dsk_ref[:, g0:g0 + gcols] * xs_g
            gated = y * _silu(z_at(rs, slice(g0, g0 + gcols)))
            if pad_scrs:
                y_ref[0, :, g0:g0 + gcols] = gated[0:valid]
            else:
                y_ref[0, rs, g0:g0 + gcols] = gated
            st_scr[:, g0:g0 + gcols] = e_last[:, g0:g0 + gcols] * st_g + _dot_tn(b_g, xde_b[:, g0:g0 + gcols])

    @pl.when(ci == pl.num_programs(1) - 1)
    def _():
        for c0 in range(0, d_inner, LANES):
            sout_ref[0, c0:c0 + LANES, :] = st_scr[:, c0:c0 + LANES].T


def _ssd(xa, z, dt, s0, a_row, dsk_e, ex, tri, n_sub):
    bsz, seq, nx = xa.shape
    d_inner = z.shape[2]
    short = seq < SSD_CHUNK
    rows = seq if short else SSD_CHUNK * n_sub
    assert seq % rows == 0 and (n_sub == 1 or not short)
    row = lambda n: pl.BlockSpec((1, rows, n), lambda b, c: (b, c, 0))
    st_spec = pl.BlockSpec((1, d_inner, D_STATE), lambda b, c: (b, 0, 0))
    consts = (a_row, dsk_e, ex, tri)
    scratch = [pltpu.VMEM((D_STATE, d_inner), F32)]
    if short:
        scratch += [pltpu.VMEM((SSD_CHUNK, n), F32) for n in (nx, d_inner, LANES)]
    return pl.pallas_call(
        functools.partial(_ssd_kernel, seq if short else SSD_CHUNK, n_sub),
        grid=(bsz, seq // rows),
        in_specs=[row(nx), row(d_inner), row(LANES), st_spec] + [_const_spec(c.shape) for c in consts],
        out_specs=[row(d_inner), st_spec],
        out_shape=[jax.ShapeDtypeStruct((bsz, seq, d_inner), F32),
                   jax.ShapeDtypeStruct((bsz, d_inner, D_STATE), F32)],
        scratch_shapes=scratch,
        compiler_params=_cparams("arbitrary", "arbitrary"),
        name="ssd_chunk",
    )(xa, z, dt, s0, *consts)


def _out_proj_kernel(pre_norm, a_ref, *refs):
    if pre_norm:
        gw_ref, w_ref, h_ref, pw_ref, o_ref = refs
        a = _rms(a_ref[...].astype(F32), gw_ref[...]).astype(BF16)
    else:
        w_ref, h_ref, pw_ref, o_ref = refs
        a = a_ref[...].astype(BF16)
    o_ref[...] = h_ref[...] + _rms(_dot(a, w_ref[...]), pw_ref[...])


def _out_proj(a, w, h, pw, tm, gw=None):
    m, k = a.shape
    d = w.shape[1]
    assert m % tm == 0
    row = lambda n: pl.BlockSpec((tm, n), lambda i: (i, 0))
    ins = [a] + ([gw] if gw is not None else []) + [w, h, pw]
    specs = [row(k)] + ([_const_spec(gw.shape)] if gw is not None else []) + \
        [_const_spec(w.shape), row(d), _const_spec(pw.shape)]
    return pl.pallas_call(
        functools.partial(_out_proj_kernel, gw is not None),
        grid=(m // tm,),
        in_specs=specs,
        out_specs=row(d),
        out_shape=jax.ShapeDtypeStruct((m, d), F32),
        compiler_params=_cparams("arbitrary"),
        name="out_proj",
    )(*ins)


def _ffn_in_seq_kernel(x_ref, nw_ref, wg_ref, wu_ref, cw_ref, cb_ref, act_ref, st_ref, xn_scr, carry_scr, gbuf_scr):
    li = pl.program_id(1)

    @pl.when(li == 0)
    def _():
        carry_scr[...] = jnp.zeros_like(carry_scr)

    xn_scr[...] = _rms(x_ref[0], nw_ref[...]).astype(BF16)
    xn = xn_scr[...]
    tm = xn.shape[0]
    for c0, cw in _col_chunks(wg_ref.shape[1]):
        g = _dot(xn, wg_ref[:, c0:c0 + cw])
        up = _dot(xn, wu_ref[:, c0:c0 + cw])
        y = _conv_rows(g, carry_scr[:, c0:c0 + cw], cw_ref[:, c0:c0 + cw], cb_ref[:, c0:c0 + cw], gbuf_scr)
        act_ref[0, :, c0:c0 + cw] = (_silu(y) * up).astype(act_ref.dtype)
        carry_scr[:, c0:c0 + cw] = g[tm - SUBLANES:]
    st_ref[0] = carry_scr[...]


def _ffn_in_seq(h, nw, wg, wu, conv_w, conv_b, tm):
    bsz, seq, d = h.shape
    nf = wg.shape[1]
    assert seq % tm == 0
    row = lambda n: pl.BlockSpec((1, tm, n), lambda b, l: (b, l, 0))
    return pl.pallas_call(
        _ffn_in_seq_kernel,
        grid=(bsz, seq // tm),
        in_specs=[row(d), _const_spec(nw.shape), _const_spec(wg.shape), _const_spec(wu.shape),
                  _const_spec(conv_w.shape), _const_spec(conv_b.shape)],
        out_specs=[row(nf), pl.BlockSpec((1, SUBLANES, nf), lambda b, l: (b, 0, 0))],
        out_shape=[jax.ShapeDtypeStruct((bsz, seq, nf), BF16),
                   jax.ShapeDtypeStruct((bsz, SUBLANES, nf), F32)],
        scratch_shapes=[pltpu.VMEM((tm, d), BF16), pltpu.VMEM((SUBLANES, nf), F32),
                        pltpu.VMEM((SUBLANES + tm, COL_CHUNK), F32)],
        compiler_params=_cparams("arbitrary", "arbitrary"),
        name="ffn_in_seq",
    )(h, nw, wg, wu, conv_w, conv_b)


def _ffn_in_step_kernel(n_new, x_ref, st_ref, nw_ref, wg_ref, wu_ref, cw_ref, cb_ref, act_ref, nst_ref):
    xn = _rms(x_ref[...], nw_ref[...]).astype(BF16)
    nb = xn.shape[0] // n_new
    width = cw_ref.shape[0]
    for c0, cw in _col_chunks(wg_ref.shape[1]):
        g = _dot(xn, wg_ref[:, c0:c0 + cw])
        up = _dot(xn, wu_ref[:, c0:c0 + cw])
        slabs = [st_ref[k, :, c0:c0 + cw] for k in range(width - 1)]
        slabs += [g[t * nb:(t + 1) * nb] for t in range(n_new)]
        outs = _conv_slabs(slabs, cw_ref[:, c0:c0 + cw], cb_ref[:, c0:c0 + cw], n_new)
        for t in range(n_new):
            act_ref[t * nb:(t + 1) * nb, c0:c0 + cw] = \
                (_silu(outs[t]) * up[t * nb:(t + 1) * nb]).astype(act_ref.dtype)
        for k in range(width - 1):
            nst_ref[k, :, c0:c0 + cw] = slabs[n_new + k]


def _ffn_in_step(x_tm, st_tm, nw, wg, wu, conv_w, conv_b, n_new):
    m, d = x_tm.shape
    nf = wg.shape[1]
    args = (x_tm, st_tm, nw, wg, wu, conv_w, conv_b)
    return pl.pallas_call(
        functools.partial(_ffn_in_step_kernel, n_new),
        grid=(1,),
        in_specs=[_const_spec(a.shape) for a in args],
        out_specs=[_const_spec((m, nf)), _const_spec(st_tm.shape)],
        out_shape=[jax.ShapeDtypeStruct((m, nf), BF16), jax.ShapeDtypeStruct(st_tm.shape, F32)],
        compiler_params=_cparams("arbitrary"),
        name="ffn_in_step",
    )(*args)


def _kvf_kernel(x_ref, nw_ref, wk_ref, wv_ref, wf_ref, bf_ref, tri_ref,
                k_ref, v_ref, kb_ref, vb_ref, lf_ref, fc_ref, xn_scr, carry_scr):
    li = pl.program_id(1)

    @pl.when(li == 0)
    def _():
        carry_scr[...] = jnp.zeros_like(carry_scr)

    xn_scr[...] = _rms(x_ref[0], nw_ref[...]).astype(BF16)
    xn = xn_scr[...]
    for c0, cw in _col_chunks(wk_ref.shape[1]):
        k = _dot(xn, wk_ref[:, c0:c0 + cw])
        k_ref[0, :, c0:c0 + cw] = k
        kb_ref[0, :, c0:c0 + cw] = k.astype(BF16)
        v = _dot(xn, wv_ref[:, c0:c0 + cw])
        v_ref[0, :, c0:c0 + cw] = v
        vb_ref[0, :, c0:c0 + cw] = v.astype(BF16)
    f = _dot(xn, wf_ref[...]) + bf_ref[...]
    lf = -_softplus(-f)
    lf_ref[0] = lf
    fc = _dot3_sel_x(tri_ref[...], lf) + carry_scr[0:1, :]
    fc_ref[0] = fc
    tm = fc.shape[0]
    carry_scr[...] = jnp.broadcast_to(fc[tm - 1:tm, :], carry_scr.shape)


def _kvf(h, nw, wk, wv, wf, bf, tri, tm):
    bsz, seq, d = h.shape
    nk = wk.shape[1]
    assert seq % tm == 0
    row = lambda n: pl.BlockSpec((1, tm, n), lambda b, l: (b, l, 0))
    ins = (h, nw, wk, wv, wf, bf, tri)
    return pl.pallas_call(
        _kvf_kernel,
        grid=(bsz, seq // tm),
        in_specs=[row(d)] + [_const_spec(a.shape) for a in ins[1:]],
        out_specs=[row(nk), row(nk), row(nk), row(nk), row(LANES), row(LANES)],
        out_shape=[jax.ShapeDtypeStruct((bsz, seq, nk), F32), jax.ShapeDtypeStruct((bsz, seq, nk), F32),
                   jax.ShapeDtypeStruct((bsz, seq, nk), BF16), jax.ShapeDtypeStruct((bsz, seq, nk), BF16),
                   jax.ShapeDtypeStruct((bsz, seq, LANES), F32), jax.ShapeDtypeStruct((bsz, seq, LANES), F32)],
        scratch_shapes=[pltpu.VMEM((tm, d), BF16), pltpu.VMEM((SUBLANES, LANES), F32)],
        compiler_params=_cparams("arbitrary", "arbitrary"),
        name="kvf_proj",
    )(*ins)


def _q_proj_kernel(scale, x_ref, nw_ref, w_ref, q_ref):
    xn = _rms(x_ref[...], nw_ref[...]).astype(BF16)
    for c0, cw in _col_chunks(w_ref.shape[1]):
        q_ref[:, c0:c0 + cw] = (_dot(xn, w_ref[:, c0:c0 + cw]) * scale).astype(q_ref.dtype)


def _q_proj(h, nw, w, scale, tm, out_dtype):
    m, d = h.shape
    n = w.shape[1]
    assert m % tm == 0
    row = lambda c: pl.BlockSpec((tm, c), lambda i: (i, 0))
    return pl.pallas_call(
        functools.partial(_q_proj_kernel, scale),
        grid=(m // tm,),
        in_specs=[row(d), _const_spec(nw.shape), _const_spec(w.shape)],
        out_specs=row(n),
        out_shape=jax.ShapeDtypeStruct((m, n), out_dtype),
        compiler_params=_cparams("arbitrary"),
        name="q_proj",
    )(h, nw, w)


LOG2E = 1.4426950408889634
N_SPLIT = 3
KV_BUILD_ROWS = 512


def _bias_selectors(hp, key_side):
    hrow = lax.broadcasted_iota(jnp.int32, (LANES, LANES), 0)
    lcol = lax.broadcasted_iota(jnp.int32, (LANES, LANES), 1)
    off = N_SPLIT if key_side else 0
    sels = []
    for i in range(2):
        per_head = []
        for r in range(N_SPLIT):
            hit = jnp.where(hrow == 2 * hp + i, lcol, -1) == 2 * N_SPLIT * i + off + r
            per_head.append(jnp.where(hit, 1.0, 0.0).astype(BF16))
        sels.append(per_head)
    lane = lax.broadcasted_iota(jnp.int32, (1, LANES), 1)
    ones = []
    for i in range(2):
        lo = 2 * N_SPLIT * i + (0 if key_side else N_SPLIT)
        ones.append(jnp.where((lane >= lo) & (lane < lo + N_SPLIT), 1.0, 0.0))
    return sels, ones


def _fox_prompt_kernel(q_ref, k_ref, v_ref, fc_ref, o_ref, kx_scr, qx_scr, sa_scr, sb_scr, m0_scr, m1_scr, l_scr,
                       acc_scr):
    hp = pl.program_id(1)
    u = pl.program_id(2)
    tq = q_ref.shape[1] // 2
    tk = tq
    seq = k_ref.shape[1]
    lane = lax.broadcasted_iota(jnp.int32, (tq, LANES), 1)
    first = lane < ATT_HEAD_DIM

    @pl.when(u == 0)
    def _():
        sels, ones = _bias_selectors(hp, True)
        both = [sels[0][r] + sels[1][r] for r in range(N_SPLIT)]
        one_row = ones[0] + ones[1]

        def build(c, carry):
            r0 = pl.multiple_of(c * KV_BUILD_ROWS, KV_BUILD_ROWS)
            parts = _split3(-(fc_ref[0, pl.ds(r0, KV_BUILD_ROWS), :] * LOG2E))
            ext = one_row + _dot(parts[0], both[0])
            for r in range(1, N_SPLIT):
                ext = ext + _dot(parts[r], both[r])
            kx_scr[pl.ds(r0, KV_BUILD_ROWS), 0:LANES] = k_ref[0, pl.ds(r0, KV_BUILD_ROWS), :]
            kx_scr[pl.ds(r0, KV_BUILD_ROWS), LANES:2 * LANES] = ext.astype(BF16)
            return carry

        lax.fori_loop(0, seq // KV_BUILD_ROWS, build, 0, unroll=2)

    sels, ones = _bias_selectors(hp, False)

    def build_qx(t):
        q = q_ref[0, t * tq:(t + 1) * tq, :]
        zq = jnp.zeros_like(q)
        q0 = pl.multiple_of((2 * u + t) * tq, tq)
        fparts = _split3(fc_ref[0, pl.ds(q0, tq), :] * LOG2E)
        for i in range(2):
            ext = ones[i] + _dot(fparts[0], sels[i][0])
            for r in range(1, N_SPLIT):
                ext = ext + _dot(fparts[r], sels[i][r])
            qh = jnp.where(first, q, zq) if i == 0 else jnp.where(first, zq, q)
            qx_scr[t, i] = jnp.concatenate([qh, ext.astype(BF16)], axis=1)

    def reset_state():
        m0_scr[...] = jnp.full_like(m0_scr, -jnp.inf)
        m1_scr[...] = jnp.full_like(m1_scr, -jnp.inf)
        l_scr[...] = jnp.zeros_like(l_scr)
        acc_scr[...] = jnp.zeros_like(acc_scr)

    def finish(t):
        o_ref[0, t * tq:(t + 1) * tq, :] = (acc_scr[...] / l_scr[...]).astype(o_ref.dtype)

    first_k = lax.broadcasted_iota(jnp.int32, (tk, LANES), 1) < ATT_HEAD_DIM
    ones_top = jnp.where(first_k, 1.0, 0.0).astype(BF16)
    ones_bot = jnp.where(first_k, 0.0, 1.0).astype(BF16)
    m_scrs = (m0_scr, m1_scr)

    def qk_into(buf, t, kj):
        start = pl.multiple_of(kj * tk, tk)
        kx = kx_scr[pl.ds(start, tk), :]
        for i in range(2):
            buf[i * tq:(i + 1) * tq, :] = _dot_nt(qx_scr[t, i], kx)

    def consume(buf, kj, masked):
        start = pl.multiple_of(kj * tk, tk)
        v = v_ref[0, pl.ds(start, tk), :]
        zv = jnp.zeros_like(v)
        vx = jnp.concatenate([jnp.concatenate([jnp.where(first_k, v, zv), ones_top], axis=1),
                              jnp.concatenate([jnp.where(first_k, zv, v), ones_bot], axis=1)], axis=0)
        ps, alphas = [], []
        for i in range(2):
            chunks = [buf[i * tq:(i + 1) * tq, c0:c0 + LANES] for c0 in range(0, tk, LANES)]
            if masked:
                r = lax.broadcasted_iota(jnp.int32, (tq, LANES), 0)
                c = lax.broadcasted_iota(jnp.int32, (tq, LANES), 1)
                chunks = [jnp.where(r >= c + c0, ch, -jnp.inf) for ch, c0 in zip(chunks, range(0, tk, LANES))]
            m_old = m_scrs[i][...]
            m_new = jnp.maximum(m_old, jnp.max(functools.reduce(jnp.maximum, chunks), axis=1, keepdims=True))
            alphas.append(jnp.exp2(m_old - m_new))
            ps.append(jnp.concatenate([jnp.exp2(ch - m_new).astype(BF16) for ch in chunks], axis=1))
            m_scrs[i][...] = m_new
        tot = _dot(jnp.concatenate(ps, axis=1), vx)
        alpha = jnp.where(first, alphas[0], alphas[1])
        acc_scr[...] = alpha * acc_scr[...] + tot[:, :LANES]
        l_scr[...] = alpha * l_scr[...] + tot[:, LANES:]

    def pair_loop(t, cur, nxt):
        def body(j, c):
            qk_into(nxt, t, 2 * j + 1)
            consume(cur, 2 * j, False)
            qk_into(cur, t, 2 * j + 2)
            consume(nxt, 2 * j + 1, False)
            return c
        lax.fori_loop(0, u, body, 0)

    build_qx(0)
    reset_state()
    qk_into(sa_scr, 0, 0)
    pair_loop(0, sa_scr, sb_scr)
    build_qx(1)
    qk_into(sb_scr, 1, 0)
    consume(sa_scr, 2 * u, True)
    finish(0)
    reset_state()
    pair_loop(1, sb_scr, sa_scr)
    qk_into(sa_scr, 1, 2 * u + 1)
    consume(sb_scr, 2 * u, False)
    consume(sa_scr, 2 * u + 1, True)
    finish(1)


def _fox_prompt(q, kb, vb, fc, tq):
    bsz, seq, dm = q.shape
    n_pairs = dm // LANES
    assert seq % (2 * tq) == 0 and seq % KV_BUILD_ROWS == 0
    n_steps = seq // (2 * tq)
    return pl.pallas_call(
        _fox_prompt_kernel,
        grid=(bsz, n_pairs, n_steps),
        in_specs=[pl.BlockSpec((1, 2 * tq, LANES), lambda b, p, i: (b, i, p)),
                  pl.BlockSpec((1, seq, LANES), lambda b, p, i: (b, 0, p)),
                  pl.BlockSpec((1, seq, LANES), lambda b, p, i: (b, 0, p)),
                  pl.BlockSpec((1, seq, LANES), lambda b, p, i: (b, 0, 0))],
        out_specs=pl.BlockSpec((1, 2 * tq, LANES), lambda b, p, i: (b, i, p)),
        out_shape=jax.ShapeDtypeStruct((bsz, seq, dm), BF16),
        scratch_shapes=[pltpu.VMEM((seq, 2 * LANES), BF16), pltpu.VMEM((2, 2, tq, 2 * LANES), BF16),
                        pltpu.VMEM((2 * tq, tq), F32), pltpu.VMEM((2 * tq, tq), F32),
                        pltpu.VMEM((tq, LANES), F32), pltpu.VMEM((tq, LANES), F32),
                        pltpu.VMEM((tq, LANES), F32), pltpu.VMEM((tq, LANES), F32)],
        compiler_params=_cparams("arbitrary", "arbitrary", "arbitrary"),
        name="fox_prompt",
    )(q, kb, vb, fc)


def _fox_sample_kernel(n_group, n_new, pt_ref, q_ref, kn_ref, vn_ref, lfn_ref, suf_ref, *refs):
    k_refs = refs[0:n_group]
    v_refs = refs[n_group:2 * n_group]
    lf_refs = refs[2 * n_group:3 * n_group]
    o_ref, m_scr, l_scr, acc_scr, fn_scr, carry_scr = refs[3 * n_group:]
    i = pl.program_id(1)
    n_heads, n_rows, dh = q_ref.shape[1:]
    page = suf_ref.shape[0]

    @pl.when(i == 0)
    def _():
        y = lfn_ref[0]
        t3 = lax.broadcasted_iota(jnp.int32, y.shape, 1)
        sh = 1
        while sh < n_new:
            y = y + jnp.where(t3 >= sh, pltpu.roll(y, sh, 1), 0.0)
            sh *= 2
        fn_scr[...] = y
        q = q_ref[0]
        t1 = lax.broadcasted_iota(jnp.int32, (n_heads, n_rows, 1), 1)
        fcol = y[:, :, 0:1]
        ss = []
        for u in range(n_new):
            s_u = jnp.sum(q * kn_ref[0, :, u:u + 1, :], axis=2, keepdims=True) + fcol - fcol[:, u:u + 1, :]
            ss.append(jnp.where(t1 >= u, s_u, -jnp.inf))
        m0 = functools.reduce(jnp.maximum, ss)
        l0 = jnp.zeros_like(m0)
        acc0 = jnp.zeros((n_heads, n_rows, dh), F32)
        for u in range(n_new):
            p_u = jnp.exp(ss[u] - m0)
            l0 = l0 + p_u
            acc0 = acc0 + p_u * vn_ref[0, :, u:u + 1, :]
        m_scr[...] = jnp.broadcast_to(m0, m_scr.shape)
        l_scr[...] = jnp.broadcast_to(l0, l_scr.shape)
        acc_scr[...] = acc0
        carry_scr[...] = jnp.zeros_like(carry_scr)

    suf = suf_ref[...]
    withins, prefixes = [], []
    run = carry_scr[:, 0:1]
    for g in range(n_group):
        lf_t = lf_refs[g][0]
        withins.append(_dot3_x_sel(lf_t, suf))
        prefixes.append(run)
        run = run + jnp.sum(lf_t, axis=1, keepdims=True)
    carry_scr[...] = jnp.broadcast_to(run, carry_scr.shape)

    shift = jnp.concatenate([withins[g] + prefixes[g] for g in range(n_group)], axis=1)
    fn = jnp.concatenate([fn_scr[...]] * n_group, axis=2)
    s3 = jnp.stack([
        _dot(q_ref[0, h].astype(BF16),
             jnp.concatenate([k_refs[g][0, h] for g in range(n_group)], axis=1).astype(BF16)) + shift[h:h + 1, :]
        for h in range(n_heads)]) + fn
    m_old = m_scr[...]
    m_new = jnp.maximum(m_old, jnp.max(s3, axis=2, keepdims=True))
    alpha = jnp.exp(m_old - m_new)
    p3 = jnp.exp(s3 - jnp.concatenate([m_new] * n_group, axis=2))
    l_scr[...] = alpha * l_scr[...] + jnp.sum(p3, axis=2, keepdims=True)
    m_scr[...] = m_new
    pb = p3.astype(BF16)
    pv = jnp.stack([
        _dot_nt(pb[h], jnp.concatenate([v_refs[g][0, h] for g in range(n_group)], axis=1).astype(BF16))
        for h in range(n_heads)])
    acc_scr[...] = alpha[:, :, 0:dh] * acc_scr[...] + pv

    @pl.when(i == pl.num_programs(1) - 1)
    def _():
        o_ref[0] = acc_scr[...] / l_scr[:, :, 0:dh]


def _fox_sample(q4, kn4, vn4, lfn4, cache_k, cache_v, cache_lf_t, page_table, suf, n_group, n_new):
    db, n_heads, n_rows, dh = q4.shape
    n_pages = page_table.shape[1]
    page = cache_lf_t.shape[2]
    assert n_pages % n_group == 0 and page == LANES and n_new <= n_rows
    pt_flat = page_table.reshape(-1)

    def page_map(g, ndim):
        def imap(b, i, pt):
            return (pt[b * n_pages + (n_pages - 1 - (i * n_group + g))],) + (0,) * (ndim - 1)
        return imap

    seq_spec = lambda c: pl.BlockSpec((1, n_heads, n_rows, c), lambda b, i, pt: (b, 0, 0, 0))
    in_specs = [seq_spec(dh), seq_spec(dh), seq_spec(dh), seq_spec(LANES),
                pl.BlockSpec(suf.shape, lambda b, i, pt: (0, 0))]
    in_specs += [pl.BlockSpec((1, n_heads, dh, page), page_map(g, 4)) for g in range(n_group)]
    in_specs += [pl.BlockSpec((1, n_heads, dh, page), page_map(g, 4)) for g in range(n_group)]
    in_specs += [pl.BlockSpec((1, n_heads, page), page_map(g, 3)) for g in range(n_group)]
    grid_spec = pltpu.PrefetchScalarGridSpec(
        num_scalar_prefetch=1,
        grid=(db, n_pages // n_group),
        in_specs=in_specs,
        out_specs=seq_spec(dh),
        scratch_shapes=[pltpu.VMEM((n_heads, n_rows, LANES), F32), pltpu.VMEM((n_heads, n_rows, LANES), F32),
                        pltpu.VMEM((n_heads, n_rows, dh), F32), pltpu.VMEM((n_heads, n_rows, LANES), F32),
                        pltpu.VMEM((n_heads, LANES), F32)],
    )
    return pl.pallas_call(
        functools.partial(_fox_sample_kernel, n_group, n_new),
        grid_spec=grid_spec,
        out_shape=jax.ShapeDtypeStruct((db, n_heads, n_rows, dh), F32),
        compiler_params=_cparams("arbitrary", "arbitrary"),
        name="fox_sample",
    )(pt_flat, q4, kn4, vn4, lfn4, suf, *([cache_k] * n_group), *([cache_v] * n_group),
      *([cache_lf_t] * n_group))


def _prep_weights(norm_w, m_w_in, m_conv_w, m_conv_b, m_dt_bias, m_a_log, m_d, m_gnorm_w, m_w_out,
                  ffn_w_in, ffn_conv_w, ffn_conv_b, ffn_w_out, kv_norm_w, w_kvf, b_f, w_q, w_o):
    d_model = norm_w.shape[-1]
    d_inner = m_w_out.shape[1]
    n_sheads = m_dt_bias.shape[1]
    conv_dim = m_conv_w.shape[2]
    d_ff = ffn_w_out.shape[1]
    att_dim = w_q.shape[2]
    n_aheads = b_f.shape[0]
    pad_lanes = lambda a: jnp.pad(a, [(0, 0)] * (a.ndim - 1) + [(0, LANES - a.shape[-1])])
    w = {}
    w["norm"] = norm_w.reshape(norm_w.shape[0], norm_w.shape[1], 1, d_model)
    w["m_wz"] = m_w_in[:, :, :d_inner].astype(BF16)
    w["m_wx"] = m_w_in[:, :, d_inner:d_inner + conv_dim].astype(BF16)
    w["m_wdt"] = pad_lanes(m_w_in[:, :, d_inner + conv_dim:]).astype(BF16)
    w["m_conv_w"] = m_conv_w
    w["m_conv_b"] = m_conv_b[:, None, :]
    w["m_dtb"] = pad_lanes(m_dt_bias)[:, None, :]
    w["m_alog"] = pad_lanes(m_a_log)[:, None, :]
    w["m_dsk"] = jnp.repeat(m_d.astype(F32), SSM_HEAD_DIM, axis=1)[:, None, :]
    w["m_gw"] = m_gnorm_w[:, None, :]
    w["m_wout"] = m_w_out.astype(BF16)
    w["f_wg"] = ffn_w_in[:, :, :d_ff].astype(BF16)
    w["f_wu"] = ffn_w_in[:, :, d_ff:].astype(BF16)
    w["f_conv_w"] = ffn_conv_w
    w["f_conv_b"] = ffn_conv_b[:, None, :]
    w["f_wout"] = ffn_w_out.astype(BF16)
    w["kv_norm"] = kv_norm_w[None, :]
    w["wk"] = w_kvf[:, :att_dim].astype(BF16)
    w["wv"] = w_kvf[:, att_dim:2 * att_dim].astype(BF16)
    w["wf"] = pad_lanes(w_kvf[:, 2 * att_dim:]).astype(BF16)
    w["bf"] = pad_lanes(b_f[None, :])
    w["wq"] = w_q.astype(BF16)
    w["wo"] = w_o.astype(BF16)
    hh = jnp.arange(LANES)[:, None]
    cc = jnp.arange(d_inner)[None, :] // SSM_HEAD_DIM
    w["ex"] = (hh == cc).astype(BF16)
    r = jnp.arange(SSD_CHUNK)
    w["tri_chunk"] = (r[:, None] >= r[None, :]).astype(BF16)
    p = jnp.arange(PAGE_SIZE)
    w["suf"] = (p[:, None] > p[None, :]).astype(BF16)
    w["n_sheads"] = n_sheads
    w["n_aheads"] = n_aheads
    return w


TM_SEQ = 1024
TM_MAMBA_IN = 512
TM_KVF = 512
SSD_SUB = 4
TQ = 512
N_PAGE_GROUP = 16


def _tri(n):
    r = jnp.arange(n)
    return (r[:, None] >= r[None, :]).astype(BF16)


def _prompt_trunk(x, w, depth, n_a):
    bsz, seq, d = x.shape
    m = bsz * seq
    h = x
    d_inner = w["m_wout"].shape[1]
    outs_ssm, outs_mconv, outs_ffn = [], [], []
    kf = vf = kb = vb = lf = fc = None
    for i in range(depth):
        nw = w["norm"][i]
        if i < n_a:
            z, xa, dt, cst = _mamba_in_seq(h, nw[0], w["m_wz"][i], w["m_wx"][i], w["m_wdt"][i],
                                           w["m_conv_w"][i], w["m_conv_b"][i], w["m_dtb"][i], TM_MAMBA_IN)
            s0 = jnp.zeros((bsz, d_inner, D_STATE), F32)
            y, s_new = _ssd(xa, z, dt, s0, w["m_alog"][i], w["m_dsk"][i], w["ex"], w["tri_chunk"], SSD_SUB)
            h = _out_proj(y.reshape(m, d_inner), w["m_wout"][i], h.reshape(m, d), nw[1], TM_SEQ,
                          gw=w["m_gw"][i]).reshape(bsz, seq, d)
            width = w["m_conv_w"].shape[1]
            outs_mconv.append(cst[:, SUBLANES - (width - 1):, :])
            outs_ssm.append(s_new.reshape(bsz, w["n_sheads"], SSM_HEAD_DIM, D_STATE))
        else:
            j = i - n_a
            q = _q_proj(h.reshape(m, d), nw[0], w["wq"][j], ATT_HEAD_DIM ** -0.5 * LOG2E, TM_SEQ, BF16)
            o = _fox_prompt(q.reshape(bsz, seq, -1), kb, vb, fc, TQ)
            h = _out_proj(o.reshape(m, -1), w["wo"][j], h.reshape(m, d), nw[1], TM_SEQ).reshape(bsz, seq, d)
        act, fst = _ffn_in_seq(h, nw[2], w["f_wg"][i], w["f_wu"][i], w["f_conv_w"][i], w["f_conv_b"][i], TM_SEQ)
        h = _out_proj(act.reshape(m, -1), w["f_wout"][i], h.reshape(m, d), nw[3], TM_SEQ).reshape(bsz, seq, d)
        fwidth = w["f_conv_w"].shape[1]
        outs_ffn.append(fst[:, SUBLANES - (fwidth - 1):, :])
        if i == n_a - 1:
            kf, vf, kb, vb, lf, fc = _kvf(h, w["kv_norm"], w["wk"], w["wv"], w["wf"], w["bf"], _tri(TM_KVF), TM_KVF)
    n_ah = w["n_aheads"]
    return (h, jnp.stack(outs_ssm), jnp.stack(outs_mconv), jnp.stack(outs_ffn),
            kf.reshape(bsz, seq, n_ah, ATT_HEAD_DIM), vf.reshape(bsz, seq, n_ah, ATT_HEAD_DIM), lf[:, :, :n_ah])


def _to_tm(a):
    db, n_new, c = a.shape
    return jnp.swapaxes(a, 0, 1).reshape(n_new * db, c)


def _from_tm(a, db):
    m, c = a.shape
    return jnp.swapaxes(a.reshape(m // db, db, c), 0, 1)


def _heads_major(a, n_heads):
    db, n_new, c = a.shape
    a4 = jnp.swapaxes(a.reshape(db, n_new, n_heads, c // n_heads), 1, 2)
    return jnp.pad(a4, ((0, 0), (0, 0), (0, SUBLANES - n_new), (0, 0)))


def _sample_trunk(x, ssm_st, mconv_st, ffn_st, cache_k, cache_v, cache_logf, page_table, w, depth, n_a):
    db, n_new, d = x.shape
    m = db * n_new
    h = _to_tm(x)
    d_inner = w["m_wout"].shape[1]
    n_ah = w["n_aheads"]
    outs_ssm, outs_mconv, outs_ffn = [], [], []
    kf = vf = lf = None
    for i in range(depth):
        nw = w["norm"][i]
        if i < n_a:
            st_tm = jnp.swapaxes(mconv_st[i], 0, 1)
            z, xa, dt, nst = _mamba_in_step(h, st_tm, nw[0], w["m_wz"][i], w["m_wx"][i], w["m_wdt"][i],
                                            w["m_conv_w"][i], w["m_conv_b"][i], w["m_dtb"][i], n_new)
            s0 = ssm_st[i].reshape(db, d_inner, D_STATE)
            y, s_new = _ssd(_from_tm(xa, db), _from_tm(z, db), _from_tm(dt, db), s0,
                            w["m_alog"][i], w["m_dsk"][i], w["ex"], w["tri_chunk"], 1)
            y_tm = _to_tm(y)
            h = _out_proj(y_tm, w["m_wout"][i], h, nw[1], m, gw=w["m_gw"][i])
            outs_mconv.append(jnp.swapaxes(nst, 0, 1))
            outs_ssm.append(s_new.reshape(db, w["n_sheads"], SSM_HEAD_DIM, D_STATE))
        else:
            j = i - n_a
            q = _from_tm(_q_proj(h, nw[0], w["wq"][j], ATT_HEAD_DIM ** -0.5, m, F32), db)
            ck_t = jnp.transpose(cache_k, (0, 2, 3, 1))
            cv_t = jnp.transpose(cache_v, (0, 2, 3, 1))
            clf_t = jnp.swapaxes(cache_logf, 1, 2)
            lfn3 = jnp.pad(jnp.swapaxes(lf[:, :, :n_ah], 1, 2), ((0, 0), (0, 0), (0, SUBLANES - n_new)))
            lfn4 = jnp.broadcast_to(lfn3[..., None], (db, n_ah, SUBLANES, LANES))
            o4 = _fox_sample(_heads_major(q, n_ah), _heads_major(kf, n_ah), _heads_major(vf, n_ah), lfn4,
                             ck_t, cv_t, clf_t, page_table, w["suf"], N_PAGE_GROUP, n_new)
            o = jnp.swapaxes(o4[:, :, :n_new, :], 1, 2).reshape(db, n_new, -1)
            h = _out_proj(_to_tm(o), w["wo"][j], h, nw[1], m)
        fst_tm = jnp.swapaxes(ffn_st[i], 0, 1)
        act, nfst = _ffn_in_step(h, fst_tm, nw[2], w["f_wg"][i], w["f_wu"][i], w["f_conv_w"][i],
                                 w["f_conv_b"][i], n_new)
        h = _out_proj(act, w["f_wout"][i], h, nw[3], m)
        outs_ffn.append(jnp.swapaxes(nfst, 0, 1))
        if i == n_a - 1:
            kf_tm, vf_tm, _, _, lf_tm, _ = _kvf(h[None], w["kv_norm"], w["wk"], w["wv"], w["wf"], w["bf"],
                                                _tri(m), m)
            kf, vf, lf = _from_tm(kf_tm[0], db), _from_tm(vf_tm[0], db), _from_tm(lf_tm[0], db)
    return (_from_tm(h, db), jnp.stack(outs_ssm), jnp.stack(outs_mconv), jnp.stack(outs_ffn),
            kf.reshape(db, n_new, n_ah, ATT_HEAD_DIM), vf.reshape(db, n_new, n_ah, ATT_HEAD_DIM), lf[:, :, :n_ah])


def kernel(x_prompt, x_sample, state_ssm, state_mconv, state_ffn, cache_k, cache_v, cache_logf, page_table,
           norm_w, m_w_in, m_conv_w, m_conv_b, m_dt_bias, m_a_log, m_d, m_gnorm_w, m_w_out,
           ffn_w_in, ffn_conv_w, ffn_conv_b, ffn_w_out, kv_norm_w, w_kvf, b_f, w_q, w_o):
    depth = norm_w.shape[0]
    n_a = m_w_in.shape[0]
    w = _prep_weights(norm_w, m_w_in, m_conv_w, m_conv_b, m_dt_bias, m_a_log, m_d, m_gnorm_w, m_w_out,
                      ffn_w_in, ffn_conv_w, ffn_conv_b, ffn_w_out, kv_norm_w, w_kvf, b_f, w_q, w_o)
    y_p, p_ssm, p_mconv, p_ffn, p_k, p_v, p_lf = _prompt_trunk(x_prompt, w, depth, n_a)
    y_s, s_ssm, s_mconv, s_ffn, s_k, s_v, s_lf = _sample_trunk(
        x_sample, state_ssm, state_mconv, state_ffn, cache_k, cache_v, cache_logf, page_table, w, depth, n_a)
    return (y_p, y_s, p_ssm, p_mconv, p_ffn, p_k, p_v, p_lf, s_ssm, s_mconv, s_ffn, s_k, s_v, s_lf)
```
